```python
import numpy as np
import jax, jax.numpy as jnp
from jax import lax

D_MODEL = 2048
BATCH = 1
SEQ = 16384
DEPTH = 4

CTX_LEN = 256
GRID_W = 64
ROPE_THETA = 10000.0
NORM_EPS = 1e-6
N_MOD = 6

MLA_HEADS = 8
MLA_Q_RANK = 512
MLA_KV_RANK = 256
MLA_NOPE = 128
MLA_ROPE = 64
MLA_V = 128
MLA_WIDTH = MLA_HEADS * MLA_V
MLA_SCALE = (MLA_NOPE + MLA_ROPE) ** -0.5
ATTN_Q_BLOCK = 128

SWA_HEADS = 8
SWA_KV_HEADS = 2
SWA_GROUP = SWA_HEADS // SWA_KV_HEADS
SWA_HEAD_DIM = 64
SWA_WINDOW = 128
SWA_BLOCK = 128
SWA_WIDTH = SWA_HEADS * SWA_HEAD_DIM
SWA_KV_WIDTH = SWA_KV_HEADS * SWA_HEAD_DIM
SWA_SCALE = SWA_HEAD_DIM ** -0.5

CONV_WIDTH = 512
CONV_K = 3

MIX_WIDTH = MLA_WIDTH + SWA_WIDTH + CONV_WIDTH
IN_SIZES = (MLA_Q_RANK, MLA_KV_RANK, MLA_ROPE, SWA_WIDTH, SWA_KV_WIDTH, SWA_KV_WIDTH, CONV_WIDTH, CONV_WIDTH, CONV_WIDTH)
N_IN = sum(IN_SIZES)
IN_SPLIT_IDX = tuple(int(i) for i in np.cumsum(IN_SIZES)[:-1])

N_EXPERTS = 32
TOP_K = 4
D_EXPERT = 512
SWIGLU_LIMIT = 7.0
SWIGLU_ALPHA = 1.702
MOE_BLOCK = 128

kernel_name = "hybrid_mla_swa_conv_moe_dit"


def rmsnorm(x, g=None):
    x32 = x.astype(jnp.float32)
    y = x32 * lax.rsqrt(jnp.mean(x32 * x32, axis=-1, keepdims=True) + NORM_EPS)
    if g is not None:
        y = y * g.astype(jnp.float32)
    return y.astype(x.dtype)


def modulate(h, shift, scale):
    return h * (1.0 + scale) + shift


def axial_rope_tables(n_tokens, rot_dim):
    rows = n_tokens // GRID_W
    row_id, col_id = jnp.meshgrid(jnp.arange(rows), jnp.arange(GRID_W), indexing="ij")
    half = rot_dim // 2
    inv_freq = ROPE_THETA ** (-jnp.arange(0, half, 2, dtype=jnp.float32) / half)

    def axis_angles(pos):
        a = pos.reshape(-1).astype(jnp.float32)[:, None] * inv_freq[None, :]
        return jnp.concatenate([a, a], axis=-1)

    ang = jnp.concatenate([axis_angles(row_id), axis_angles(col_id)], axis=-1)
    return jnp.cos(ang), jnp.sin(ang)


def _rotate_half(u):
    u1, u2 = jnp.split(u, 2, axis=-1)
    return jnp.concatenate([-u2, u1], axis=-1)


def apply_axial_rope(x, cos, sin):
    xr, xc = jnp.split(x, 2, axis=-1)
    rot = jnp.concatenate([_rotate_half(xr), _rotate_half(xc)], axis=-1)
    return (x * cos + rot * sin).astype(x.dtype)


def mla_q(cq, g_q, w_uq):
    B, n, _ = cq.shape
    q = (rmsnorm(cq, g_q) @ w_uq).reshape(B, n, MLA_HEADS, MLA_NOPE + MLA_ROPE)
    return q[..., :MLA_NOPE], q[..., MLA_NOPE:]


def mla_kv(ckv, g_kv, w_ukv):
    B, n, _ = ckv.shape
    kv = (rmsnorm(ckv, g_kv) @ w_ukv).reshape(B, n, MLA_HEADS, MLA_NOPE + MLA_V)
    return kv[..., :MLA_NOPE], kv[..., MLA_NOPE:]


def mla_attend(qn, qr, kn, kr, v):
    s = (jnp.einsum("bqhd,bkhd->bhqk", qn, kn, preferred_element_type=jnp.float32)
         + jnp.einsum("bqhr,bkr->bhqk", qr, kr, preferred_element_type=jnp.float32)) * MLA_SCALE
    p = jax.nn.softmax(s, axis=-1).astype(v.dtype)
    return jnp.einsum("bhqk,bkhd->bqhd", p, v)


def mla_latent(qn, qr, kn, kr, v):
    B, L = qn.shape[:2]
    nb = L // ATTN_Q_BLOCK

    def to_blocks(t):
        return jnp.moveaxis(t.reshape(B, nb, ATTN_Q_BLOCK, *t.shape[2:]), 1, 0)

    out = lax.map(lambda qs: mla_attend(qs[0], qs[1], kn, kr, v), (to_blocks(qn), to_blocks(qr)))
    return jnp.moveaxis(out, 0, 1).reshape(B, L, MLA_WIDTH)


def neighbour_blocks(t, nb):
    B = t.shape[0]
    tp = jnp.pad(t, ((0, 0), (SWA_BLOCK, SWA_BLOCK), (0, 0), (0, 0)))
    tp = tp.reshape(B, nb + 2, SWA_BLOCK, *t.shape[2:])
    return jnp.concatenate([tp[:, :-2], tp[:, 1:-1], tp[:, 2:]], axis=2)


def swa_latent(q, k, v, kc, vc, sink):
    B, L = q.shape[:2]
    nb = L // SWA_BLOCK
    n_ctx = kc.shape[1]
    qb = q.reshape(B, nb, SWA_BLOCK, SWA_KV_HEADS, SWA_GROUP, SWA_HEAD_DIM)
    kb = neighbour_blocks(k, nb)
    vb = neighbour_blocks(v, nb)
    blk = jnp.arange(nb)[:, None] * SWA_BLOCK
    q_pos = blk + jnp.arange(SWA_BLOCK)[None, :]
    k_pos = blk - SWA_BLOCK + jnp.arange(3 * SWA_BLOCK)[None, :]
    kp = k_pos[:, None, :]
    valid = (jnp.abs(kp - q_pos[:, :, None]) <= SWA_WINDOW) & (kp >= 0) & (kp < L)
    s_loc = jnp.einsum("bnqhgd,bnkhd->bnhgqk", qb, kb, preferred_element_type=jnp.float32) * SWA_SCALE
    s_loc = jnp.where(valid[None, :, None, None], s_loc, -jnp.inf)
    s_ctx = jnp.einsum("bnqhgd,bkhd->bnhgqk", qb, kc, preferred_element_type=jnp.float32) * SWA_SCALE
    s_sink = jnp.broadcast_to(
        sink.astype(jnp.float32).reshape(1, 1, SWA_KV_HEADS, SWA_GROUP, 1, 1),
        s_loc.shape[:-1] + (1,))
    p = jax.nn.softmax(jnp.concatenate([s_ctx, s_loc, s_sink], axis=-1), axis=-1).astype(v.dtype)
    o = (jnp.einsum("bnhgqk,bkhd->bnqhgd", p[..., :n_ctx], vc)
         + jnp.einsum("bnhgqk,bnkhd->bnqhgd", p[..., n_ctx:-1], vb))
    return o.reshape(B, L, SWA_WIDTH)


def swa_context(q, k, v, sink):
    B, n = q.shape[:2]
    qg = q.reshape(B, n, SWA_KV_HEADS, SWA_GROUP, SWA_HEAD_DIM)
    s = jnp.einsum("bqhgd,bkhd->bhgqk", qg, k, preferred_element_type=jnp.float32) * SWA_SCALE
    s_sink = jnp.broadcast_to(
        sink.astype(jnp.float32).reshape(1, SWA_KV_HEADS, SWA_GROUP, 1, 1),
        s.shape[:-1] + (1,))
    p = jax.nn.softmax(jnp.concatenate([s, s_sink], axis=-1), axis=-1)[..., :-1].astype(v.dtype)
    return jnp.einsum("bhgqk,bkhd->bqhgd", p, v).reshape(B, n, SWA_WIDTH)


def short_conv(gate_b, gate_c, u, w):
    v = gate_c * u
    n = v.shape[1]
    pad = CONV_K // 2
    vp = jnp.pad(v, ((0, 0), (pad, pad), (0, 0)))
    y = vp[:, 0:n] * w[0]
    for j in range(1, CONV_K):
        y = y + vp[:, j:j + n] * w[j]
    return gate_b * y


def merge_groups(a, b, c, g):
    return jnp.concatenate([rmsnorm(a), rmsnorm(b), rmsnorm(c)], axis=-1) * g.astype(a.dtype)


def token_mixers(pc, pl, rope_mla, rope_swa, g_q, g_kv, w_uq, w_ukv, sink, conv_w, g_out, need_ctx):
    cq_c, ckv_c, kr_c, sq_c, sk_c, sv_c, cb_c, cc_c, cx_c = pc
    cq_l, ckv_l, kr_l, sq_l, sk_l, sv_l, cb_l, cc_l, cx_l = pl
    cos_m, sin_m = rope_mla
    cos_s, sin_s = rope_swa
    B, Lc = cq_c.shape[:2]
    L = cq_l.shape[1]

    kn_c, v_c = mla_kv(ckv_c, g_kv, w_ukv)
    kn_l, v_l = mla_kv(ckv_l, g_kv, w_ukv)
    qn_l, qr_l = mla_q(cq_l, g_q, w_uq)
    qr_l = apply_axial_rope(qr_l, cos_m[:, None], sin_m[:, None])
    kr_l = apply_axial_rope(kr_l, cos_m, sin_m)
    mla_l = mla_latent(qn_l, qr_l,
                       jnp.concatenate([kn_c, kn_l], axis=1),
                       jnp.concatenate([kr_c, kr_l], axis=1),
                       jnp.concatenate([v_c, v_l], axis=1))

    q_l = apply_axial_rope(sq_l.reshape(B, L, SWA_HEADS, SWA_HEAD_DIM), cos_s[:, None], sin_s[:, None])
    k_l = apply_axial_rope(sk_l.reshape(B, L, SWA_KV_HEADS, SWA_HEAD_DIM), cos_s[:, None], sin_s[:, None])
    vs_l = sv_l.reshape(B, L, SWA_KV_HEADS, SWA_HEAD_DIM)
    k_c = sk_c.reshape(B, Lc, SWA_KV_HEADS, SWA_HEAD_DIM)
    vs_c = sv_c.reshape(B, Lc, SWA_KV_HEADS, SWA_HEAD_DIM)
    swa_l = swa_latent(q_l, k_l, vs_l, k_c, vs_c, sink)

    conv_l = short_conv(cb_l, cc_l, cx_l, conv_w)
    y_l = merge_groups(mla_l, swa_l, conv_l, g_out)
    if not need_ctx:
        return None, y_l

    qn_c, qr_c = mla_q(cq_c, g_q, w_uq)
    mla_c = mla_attend(qn_c, qr_c, kn_c, kr_c, v_c).reshape(B, Lc, MLA_WIDTH)
    swa_c = swa_context(sq_c.reshape(B, Lc, SWA_HEADS, SWA_HEAD_DIM), k_c, vs_c, sink)
    conv_c = short_conv(cb_c, cc_c, cx_c, conv_w)
    y_c = merge_groups(mla_c, swa_c, conv_c, g_out)
    return y_c, y_l


def moe_ffn(h, w_r, b_r, w1, b1, w2, b2):
    B, n, D = h.shape
    x = h.reshape(B * n, D)
    T = x.shape[0]
    logits = jnp.dot(x, w_r, preferred_element_type=jnp.float32) + b_r.astype(jnp.float32)
    top_logit, top_e = lax.top_k(logits, TOP_K)
    gates = jax.nn.softmax(top_logit, axis=-1).astype(x.dtype)
    n_assign = T * TOP_K
    e_flat = top_e.reshape(-1)
    tok_flat = jnp.repeat(jnp.arange(T, dtype=jnp.int32), TOP_K)
    order = jnp.argsort(e_flat)
    e_sorted = e_flat[order]
    counts = jnp.zeros((N_EXPERTS,), jnp.int32).at[e_flat].add(1)
    padded = (counts + MOE_BLOCK - 1) // MOE_BLOCK * MOE_BLOCK
    start = jnp.cumsum(counts) - counts
    padded_end = jnp.cumsum(padded)
    padded_start = padded_end - padded
    dest = padded_start[e_sorted] + jnp.arange(n_assign, dtype=jnp.int32) - start[e_sorted]
    n_blocks = -(-n_assign // MOE_BLOCK) + N_EXPERTS
    n_slots = n_blocks * MOE_BLOCK
    slot_tok = jnp.full((n_slots,), T, jnp.int32).at[dest].set(tok_flat[order])
    slot_gate = jnp.zeros((n_slots,), x.dtype).at[dest].set(gates.reshape(-1)[order])
    block_start = jnp.arange(n_blocks, dtype=jnp.int32) * MOE_BLOCK
    block_expert = jnp.minimum(jnp.searchsorted(padded_end, block_start, side="right"), N_EXPERTS - 1)
    x_pad = jnp.concatenate([x, jnp.zeros((1, D), x.dtype)], axis=0)

    def expert_block(args):
        idx, e = args
        gu = x_pad[idx] @ w1[e] + b1[e]
        gate = jnp.minimum(gu[:, 0::2], SWIGLU_LIMIT)
        up = jnp.clip(gu[:, 1::2], -SWIGLU_LIMIT, SWIGLU_LIMIT)
        act = gate * jax.nn.sigmoid(SWIGLU_ALPHA * gate) * (up + 1.0)
        return act @ w2[e] + b2[e]

    y = lax.map(expert_block, (slot_tok.reshape(n_blocks, MOE_BLOCK), block_expert))
    y = y.reshape(n_slots, D) * slot_gate[:, None]
    out = jnp.zeros((T + 1, D), y.dtype).at[slot_tok].add(y)[:T]
    return out.reshape(B, n, D).astype(h.dtype)


def setup_inputs(seed: int = 0) -> dict:
    key = jax.random.key(seed)
    ks = jax.random.split(key, 24)
    f32 = jnp.float32
    D = D_MODEL

    def normal(k, shape, scale):
        return jax.random.normal(k, shape, f32) * scale

    def gain(k, shape):
        return 1.0 + 0.02 * jax.random.normal(k, shape, f32)

    return {
        "x": normal(ks[0], (BATCH, SEQ, D), 1.0),
        "c": normal(ks[1], (BATCH, D), 1.0),
        "ctx": normal(ks[2], (BATCH, CTX_LEN, D), 1.0),
        "c_ctx": normal(ks[3], (D,), 1.0),
        "w_ada": normal(ks[4], (DEPTH, D, N_MOD * D), 0.5 * D ** -0.5),
        "b_ada": normal(ks[5], (DEPTH, N_MOD * D), 0.02),
        "g_mix": gain(ks[6], (DEPTH, D)),
        "w_in": normal(ks[7], (DEPTH, D, N_IN), D ** -0.5),
        "g_mla_q": gain(ks[8], (DEPTH, MLA_Q_RANK)),
        "g_mla_kv": gain(ks[9], (DEPTH, MLA_KV_RANK)),
        "w_mla_uq": normal(ks[10], (DEPTH, MLA_Q_RANK, MLA_HEADS * (MLA_NOPE + MLA_ROPE)), MLA_Q_RANK ** -0.5),
        "w_mla_ukv": normal(ks[11], (DEPTH, MLA_KV_RANK, MLA_HEADS * (MLA_NOPE + MLA_V)), MLA_KV_RANK ** -0.5),
        "swa_sink": normal(ks[12], (DEPTH, SWA_HEADS), 0.5),
        "conv_w": normal(ks[13], (DEPTH, CONV_K, CONV_WIDTH), CONV_K ** -0.5),
        "g_out": gain(ks[14], (DEPTH, MIX_WIDTH)),
        "w_out": normal(ks[15], (DEPTH, MIX_WIDTH, D), MIX_WIDTH ** -0.5),
        "g_ffn": gain(ks[16], (DEPTH, D)),
        "w_router": normal(ks[17], (DEPTH, D, N_EXPERTS), D ** -0.5),
        "b_router": normal(ks[18], (DEPTH, N_EXPERTS), 0.01),
        "w_exp1": normal(ks[19], (DEPTH, N_EXPERTS, D, 2 * D_EXPERT), D ** -0.5),
        "b_exp1": normal(ks[20], (DEPTH, N_EXPERTS, 2 * D_EXPERT), 0.01),
        "w_exp2": normal(ks[21], (DEPTH, N_EXPERTS, D_EXPERT, D), D_EXPERT ** -0.5),
        "b_exp2": normal(ks[22], (DEPTH, N_EXPERTS, D), 0.01),
        "g_final": gain(ks[23], (D,)),
    }


def reference(x, c, ctx, c_ctx, w_ada, b_ada, g_mix, w_in, g_mla_q, g_mla_kv, w_mla_uq, w_mla_ukv,
              swa_sink, conv_w, g_out, w_out, g_ffn, w_router, b_router, w_exp1, b_exp1, w_exp2,
              b_exp2, g_final):
    B, L, D = x.shape
    n_ctx = ctx.shape[1]
    rope_mla = axial_rope_tables(L, MLA_ROPE)
    rope_swa = axial_rope_tables(L, SWA_HEAD_DIM)
    silu_c = jax.nn.silu(c)
    silu_cc = jax.nn.silu(c_ctx)
    xc, xl = ctx, x
    for l in range(DEPTH):
        need_ctx = l < DEPTH - 1
        m_x = (silu_c @ w_ada[l] + b_ada[l]).reshape(B, N_MOD, 1, D)
        m_c = (silu_cc @ w_ada[l] + b_ada[l]).reshape(N_MOD, 1, D)

        hc = modulate(rmsnorm(xc, g_mix[l]), m_c[0], m_c[1])
        hl = modulate(rmsnorm(xl, g_mix[l]), m_x[:, 0], m_x[:, 1])
        pc = jnp.split(hc @ w_in[l], IN_SPLIT_IDX, axis=-1)
        pl = jnp.split(hl @ w_in[l], IN_SPLIT_IDX, axis=-1)
        y_c, y_l = token_mixers(pc, pl, rope_mla, rope_swa, g_mla_q[l], g_mla_kv[l], w_mla_uq[l],
                                w_mla_ukv[l], swa_sink[l], conv_w[l], g_out[l], need_ctx)
        xl = xl + m_x[:, 2] * (y_l @ w_out[l])
        if need_ctx:
            xc = xc + m_c[2] * (y_c @ w_out[l])

        h2l = modulate(rmsnorm(xl, g_ffn[l]), m_x[:, 3], m_x[:, 4])
        if need_ctx:
            h2c = modulate(rmsnorm(xc, g_ffn[l]), m_c[3], m_c[4])
            f = moe_ffn(jnp.concatenate([h2c, h2l], axis=1), w_router[l], b_router[l],
                        w_exp1[l], b_exp1[l], w_exp2[l], b_exp2[l])
            xc = xc + m_c[5] * f[:, :n_ctx]
            xl = xl + m_x[:, 5] * f[:, n_ctx:]
        else:
            f = moe_ffn(h2l, w_router[l], b_router[l], w_exp1[l], b_exp1[l], w_exp2[l], b_exp2[l])
            xl = xl + m_x[:, 5] * f
    return rmsnorm(xl, g_final)
```

```python
import functools

import numpy as np
import jax
import jax.numpy as jnp
from jax import lax
from jax.experimental import pallas as pl
from jax.experimental.pallas import tpu as pltpu

F32 = jnp.float32
BF16 = jnp.bfloat16

GRID_W = 64
ROPE_THETA = 10000.0
NORM_EPS = 1e-6
N_MOD = 6
MLA_HEADS = 8
MLA_Q_RANK = 512
MLA_KV_RANK = 256
MLA_NOPE = 128
MLA_ROPE = 64
MLA_V = 128
MLA_WIDTH = MLA_HEADS * MLA_V
MLA_SCALE = (MLA_NOPE + MLA_ROPE) ** -0.5
SWA_HEADS = 8
SWA_KV_HEADS = 2
SWA_GROUP = SWA_HEADS // SWA_KV_HEADS
SWA_HEAD_DIM = 64
SWA_WINDOW = 128
SWA_WIDTH = SWA_HEADS * SWA_HEAD_DIM
SWA_KV_WIDTH = SWA_KV_HEADS * SWA_HEAD_DIM
SWA_SCALE = SWA_HEAD_DIM ** -0.5
CONV_WIDTH = 512
CONV_K = 3
MIX_WIDTH = MLA_WIDTH + SWA_WIDTH + CONV_WIDTH
N_EXPERTS = 32
TOP_K = 4
D_EXPERT = 512
SWIGLU_LIMIT = 7.0
SWIGLU_ALPHA = 1.702
LOG2E = 1.4426950408889634

LANES = 128
SUBLANES = 8
VMEM_LIMIT = 56 * 1024 * 1024

ROW_TILE = 256
SWA_TILE = 128
MOE_ROWS = 256

_O_CQ = 0
_O_CKV = _O_CQ + MLA_Q_RANK
_O_SQ = _O_CKV + MLA_KV_RANK
_O_SK = _O_SQ + SWA_WIDTH
_O_SV = _O_SK + SWA_KV_WIDTH
_O_CB = _O_SV + SWA_KV_WIDTH
_O_CC = _O_CB + CONV_WIDTH
_O_CX = _O_CC + CONV_WIDTH
_O_KR = _O_CX + CONV_WIDTH
N_IN_PAD = _O_KR + LANES


def _cparams(sem):
    return pltpu.CompilerParams(dimension_semantics=sem, vmem_limit_bytes=VMEM_LIMIT)


def _rms(x):
    return x * lax.rsqrt(jnp.mean(x * x, axis=-1, keepdims=True) + NORM_EPS)


def _rope(u, cos, sin_signed):
    w = u.shape[-1]
    reps = w // LANES
    if reps > 1:
        cos = jnp.concatenate([cos] * reps, axis=1)
        sin_signed = jnp.concatenate([sin_signed] * reps, axis=1)
    lane = lax.broadcasted_iota(jnp.int32, u.shape, 1)
    first = (lane % 32) < 16
    rot = jnp.where(first, pltpu.roll(u, w - 16, 1), pltpu.roll(u, 16, 1))
    return u * cos + rot * sin_signed


def _ada_kernel(s_ref, w_ref, b_ref, o_ref, *, chunk):
    d, tn = w_ref.shape

    def body(i, accs):
        a0, a1 = accs
        r0 = pl.multiple_of(i * chunk, chunk)
        w = w_ref[pl.ds(r0, chunk), :]
        s = s_ref[pl.ds(r0, chunk), :]
        s = s * jax.nn.sigmoid(s)
        p0 = (w * s[:, 0:1]).reshape(chunk // SUBLANES, SUBLANES, tn).sum(axis=0)
        p1 = (w * s[:, 1:2]).reshape(chunk // SUBLANES, SUBLANES, tn).sum(axis=0)
        return a0 + p0, a1 + p1

    z = jnp.zeros((SUBLANES, tn), F32)
    a0, a1 = lax.fori_loop(0, d // chunk, body, (z, z))
    b = b_ref[...]
    o_ref[0:1, :] = jnp.sum(a0, axis=0, keepdims=True) + b
    o_ref[1:2, :] = jnp.sum(a1, axis=0, keepdims=True) + b


def _ada(c, c_ctx, w_ada, b_ada):
    depth, d, n = w_ada.shape
    tn = 1024 if n % 1024 == 0 else 512
    chunk = 64
    s = jnp.stack([c.reshape(d), c_ctx.reshape(d)], axis=1)
    return pl.pallas_call(
        functools.partial(_ada_kernel, chunk=chunk),
        grid=(depth, n // tn),
        in_specs=[
            pl.BlockSpec((d, 2), lambda l, j: (0, 0)),
            pl.BlockSpec((None, d, tn), lambda l, j: (l, 0, j)),
            pl.BlockSpec((None, 1, tn), lambda l, j: (l, 0, j)),
        ],
        out_specs=pl.BlockSpec((None, 2, tn), lambda l, j: (l, 0, j)),
        out_shape=jax.ShapeDtypeStruct((depth, 2, n), F32),
        compiler_params=_cparams(("arbitrary", "arbitrary")),
        name="ada_mod",
    )(s, w_ada, b_ada.reshape(depth, 1, n))


def _inproj_kernel(x_ref, g_ref, mod_ref, cos_ref, sin_ref, w_ref,
                   cq_ref, ckv_ref, kr_ref, sq_ref, sk_ref, sv_ref, cb_ref, cv_ref):
    x = x_ref[...]
    h = _rms(x) * g_ref[...]
    h = h * (1.0 + mod_ref[1:2, :]) + mod_ref[0:1, :]
    hb = h.astype(BF16)
    cos = cos_ref[...]
    sin = sin_ref[...]

    def proj(a, width):
        return jnp.dot(hb, w_ref[:, a:a + width], preferred_element_type=F32)

    cq_ref[...] = proj(_O_CQ, MLA_Q_RANK)
    ckv_ref[...] = proj(_O_CKV, MLA_KV_RANK)
    kr_ref[...] = _rope(proj(_O_KR, LANES), cos, sin).astype(BF16)
    sq_ref[...] = (_rope(proj(_O_SQ, SWA_WIDTH), cos, sin) * SWA_SCALE).astype(BF16)
    sk_ref[...] = _rope(proj(_O_SK, SWA_KV_WIDTH), cos, sin).astype(BF16)
    sv_ref[...] = proj(_O_SV, SWA_KV_WIDTH).astype(BF16)
    cb_ref[...] = proj(_O_CB, CONV_WIDTH)
    cv_ref[...] = proj(_O_CC, CONV_WIDTH) * proj(_O_CX, CONV_WIDTH)


def _inproj(xs, g, mod, cos, sin, w_in_p, n_lat_tiles):
    t, d = xs.shape
    tm = ROW_TILE
    row = lambda w: pl.BlockSpec((tm, w), lambda i: (i, 0))
    outs = [
        (MLA_Q_RANK, F32), (MLA_KV_RANK, F32), (LANES, BF16), (SWA_WIDTH, BF16),
        (SWA_KV_WIDTH, BF16), (SWA_KV_WIDTH, BF16), (CONV_WIDTH, F32), (CONV_WIDTH, F32),
    ]
    return pl.pallas_call(
        _inproj_kernel,
        grid=(t // tm,),
        in_specs=[
            row(d),
            pl.BlockSpec((1, d), lambda i: (0, 0)),
            pl.BlockSpec((None, N_MOD, d), lambda i: (jnp.where(i < n_lat_tiles, 0, 1), 0, 0)),
            row(LANES), row(LANES),
            pl.BlockSpec((d, N_IN_PAD), lambda i: (0, 0)),
        ],
        out_specs=[row(w) for w, _ in outs],
        out_shape=[jax.ShapeDtypeStruct((t, w), dt) for w, dt in outs],
        compiler_params=_cparams(("arbitrary",)),
        name="mixer_in_proj",
    )(xs, g, mod, cos, sin, w_in_p)


def _mla_up_kernel(cq_ref, ckv_ref, kr_ref, gq_ref, gkv_ref, cos_ref, sin_ref,
                   wqn_ref, wqr_ref, wkv_ref, q_ref, k_ref, v_ref):
    hq = (_rms(cq_ref[...]) * gq_ref[...]).astype(BF16)
    qs = MLA_SCALE * LOG2E
    qn = jnp.dot(hq, wqn_ref[...], preferred_element_type=F32) * qs
    qr = _rope(jnp.dot(hq, wqr_ref[...], preferred_element_type=F32), cos_ref[...], sin_ref[...]) * qs
    hk = (_rms(ckv_ref[...]) * gkv_ref[...]).astype(BF16)
    kv = jnp.dot(hk, wkv_ref[...], preferred_element_type=F32)
    kr = kr_ref[...]
    for h in range(MLA_HEADS):
        q_ref[h, :, 0:LANES] = qn[:, h * LANES:(h + 1) * LANES].astype(BF16)
        q_ref[h, :, LANES:2 * LANES] = qr[:, h * LANES:(h + 1) * LANES].astype(BF16)
        k_ref[h, :, 0:LANES] = kv[:, h * 2 * LANES:h * 2 * LANES + LANES].astype(BF16)
        k_ref[h, :, LANES:2 * LANES] = kr
        v_ref[h] = kv[:, h * 2 * LANES + LANES:(h + 1) * 2 * LANES].astype(BF16)


def _mla_up(cq, ckv, kr, gq, gkv, cos, sin, wqn, wqr, wkv):
    t = cq.shape[0]
    tm = ROW_TILE
    row = lambda w: pl.BlockSpec((tm, w), lambda i: (i, 0))
    full = lambda a: pl.BlockSpec(a.shape, lambda i: (0,) * a.ndim)
    hd = lambda w: pl.BlockSpec((MLA_HEADS, tm, w), lambda i: (0, i, 0))
    return pl.pallas_call(
        _mla_up_kernel,
        grid=(t // tm,),
        in_specs=[row(MLA_Q_RANK), row(MLA_KV_RANK), row(LANES), full(gq), full(gkv),
                  row(LANES), row(LANES), full(wqn), full(wqr), full(wkv)],
        out_specs=[hd(2 * LANES), hd(2 * LANES), hd(MLA_V)],
        out_shape=[jax.ShapeDtypeStruct((MLA_HEADS, t, 2 * LANES), BF16),
                   jax.ShapeDtypeStruct((MLA_HEADS, t, 2 * LANES), BF16),
                   jax.ShapeDtypeStruct((MLA_HEADS, t, MLA_V), BF16)],
        compiler_params=_cparams(("arbitrary",)),
        name="mla_up_proj",
    )(cq, ckv, kr, gq, gkv, cos, sin, wqn, wqr, wkv)


def _mla_attn_kernel(q_ref, k_ref, v_ref, o_ref, m_sc, l_sc, acc_sc):
    kk = pl.program_id(2)

    @pl.when(kk == 0)
    def _():
        m_sc[...] = jnp.full(m_sc.shape, -jnp.inf, F32)
        l_sc[...] = jnp.zeros(l_sc.shape, F32)
        acc_sc[...] = jnp.zeros(acc_sc.shape, F32)

    s = lax.dot_general(q_ref[...], k_ref[...], (((1,), (1,)), ((), ())), preferred_element_type=F32)
    m_prev = m_sc[...]
    m_new = jnp.maximum(m_prev, jnp.max(s, axis=-1, keepdims=True))
    alpha = jnp.exp2(m_prev - m_new)
    p = jnp.exp2(s - m_new)
    l_sc[...] = alpha * l_sc[...] + jnp.sum(p, axis=-1, keepdims=True)
    acc_sc[...] = alpha * acc_sc[...] + jnp.dot(p.astype(BF16), v_ref[...], preferred_element_type=F32)
    m_sc[...] = m_new

    @pl.when(kk == pl.num_programs(2) - 1)
    def _():
        o_ref[...] = acc_sc[...] / l_sc[...]


def _mla_attn_call(q, k, v, prev_out, *, tq, tk, nq, nk, q_off, k_off, t_out):
    in_specs = [
        pl.BlockSpec((None, tq, 2 * LANES), lambda h, i, j: (h, i + q_off, 0)),
        pl.BlockSpec((None, tk, 2 * LANES), lambda h, i, j: (h, j + k_off, 0)),
        pl.BlockSpec((None, tk, MLA_V), lambda h, i, j: (h, j + k_off, 0)),
    ]
    args = [q, k, v]
    aliases = {}
    if prev_out is not None:
        in_specs.append(pl.BlockSpec(memory_space=pl.ANY))
        args.append(prev_out)
        aliases = {3: 0}

    def body(q_ref, k_ref, v_ref, *rest):
        _mla_attn_kernel(q_ref, k_ref, v_ref, *rest[-4:])

    return pl.pallas_call(
        body,
        grid=(MLA_HEADS, nq, nk),
        in_specs=in_specs,
        out_specs=pl.BlockSpec((tq, MLA_V), lambda h, i, j: (i + q_off, h)),
        out_shape=jax.ShapeDtypeStruct((t_out, MLA_WIDTH), F32),
        scratch_shapes=[pltpu.VMEM((tq, 1), F32), pltpu.VMEM((tq, 1), F32), pltpu.VMEM((tq, MLA_V), F32)],
        input_output_aliases=aliases,
        compiler_params=_cparams(("arbitrary", "arbitrary", "arbitrary")),
        name="mla_attention" if prev_out is None else "mla_attention_ctx",
    )(*args)


def _largest_tile(n, unit, cap):
    best = unit
    for m in range(1, n // unit + 1):
        if n % (m * unit) == 0 and m * unit <= cap:
            best = m * unit
    return best


def _mla_attention(q, k, v, n_lat, n_ctx):
    t = n_lat + n_ctx
    tq = _largest_tile(n_lat, 256, 1024)
    tk = _largest_tile(t, 256, 1536)
    out = _mla_attn_call(q, k, v, None, tq=tq, tk=tk, nq=n_lat // tq, nk=t // tk, q_off=0, k_off=0, t_out=t)
    return _mla_attn_call(q, k, v, out, tq=n_ctx, tk=n_ctx, nq=1, nk=1,
                          q_off=n_lat // n_ctx, k_off=n_lat // n_ctx, t_out=t)


def _swa_kernel(sink_ref, q_ref, kp_ref, ko_ref, kn_ref, kc_ref, vp_ref, vo_ref, vn_ref, vc_ref, o_ref,
                *, n_lat_tiles, n_lat, n_ctx):
    i = pl.program_id(0)
    tb = SWA_TILE
    nk = n_ctx + 3 * tb
    kall = jnp.concatenate([kc_ref[...], kp_ref[...], ko_ref[...], kn_ref[...]], axis=0)
    vall = jnp.concatenate([vc_ref[...], vp_ref[...], vo_ref[...], vn_ref[...]], axis=0)
    col = lax.broadcasted_iota(jnp.int32, (tb, nk), 1)
    r = lax.broadcasted_iota(jnp.int32, (tb, nk), 0)
    j = col - n_ctx
    kpos = (i - 1) * tb + j
    local_ok = (jnp.abs(j - tb - r) <= SWA_WINDOW) & (kpos >= 0) & (kpos < n_lat) & (i < n_lat_tiles)
    valid = (col < n_ctx) | local_ok
    lane_kv = lax.broadcasted_iota(jnp.int32, (nk, LANES), 1)
    k_roll = pltpu.roll(kall, SWA_HEAD_DIM, 1)
    v_roll = pltpu.roll(vall, SWA_HEAD_DIM, 1)
    gw = SWA_GROUP * SWA_HEAD_DIM
    lane_g = lax.broadcasted_iota(jnp.int32, (tb, gw), 1) // SWA_HEAD_DIM
    lane_vg = lax.broadcasted_iota(jnp.int32, (nk, gw), 1) // SWA_HEAD_DIM
    for kvh in range(SWA_KV_HEADS):
        lo = lane_kv < SWA_HEAD_DIM
        if kvh == 0:
            k2 = jnp.where(lo, kall, k_roll)
            v2 = jnp.where(lo, vall, v_roll)
        else:
            k2 = jnp.where(lo, k_roll, kall)
            v2 = jnp.where(lo, v_roll, vall)
        kt = jnp.concatenate([k2, k2], axis=1)
        vt = jnp.concatenate([v2, v2], axis=1)
        qg = q_ref[:, kvh * gw:(kvh + 1) * gw]
        acc = jnp.zeros((tb, gw), F32)
        for g in range(SWA_GROUP):
            qm = jnp.where(lane_g == g, qg, jnp.zeros_like(qg))
            s = lax.dot_general(qm, kt, (((1,), (1,)), ((), ())), preferred_element_type=F32)
            s = jnp.where(valid, s, -jnp.inf)
            sink = sink_ref[kvh * SWA_GROUP + g]
            m = jnp.maximum(jnp.max(s, axis=-1, keepdims=True), sink)
            p = jnp.exp(s - m)
            denom = jnp.sum(p, axis=-1, keepdims=True) + jnp.exp(sink - m)
            p = (p / denom).astype(BF16)
            vm = jnp.where(lane_vg == g, vt, jnp.zeros_like(vt))
            acc = acc + jnp.dot(p, vm, preferred_element_type=F32)
        o_ref[:, kvh * gw:(kvh + 1) * gw] = acc


def _swa(sink, sq, sk, sv, n_lat, n_ctx):
    t = n_lat + n_ctx
    tb = SWA_TILE
    nt = t // tb
    nlt = n_lat // tb
    kvw = SWA_KV_WIDTH
    prev_spec = pl.BlockSpec((tb, kvw), lambda i: (jnp.maximum(i - 1, 0), 0))
    own_spec = pl.BlockSpec((tb, kvw), lambda i: (i, 0))
    next_spec = pl.BlockSpec((tb, kvw), lambda i: (jnp.minimum(i + 1, nt - 1), 0))
    ctx_spec = pl.BlockSpec((n_ctx, kvw), lambda i: (n_lat // n_ctx, 0))
    return pl.pallas_call(
        functools.partial(_swa_kernel, n_lat_tiles=nlt, n_lat=n_lat, n_ctx=n_ctx),
        grid=(nt,),
        in_specs=[pl.BlockSpec(memory_space=pltpu.SMEM),
                  pl.BlockSpec((tb, SWA_WIDTH), lambda i: (i, 0)),
                  prev_spec, own_spec, next_spec, ctx_spec,
                  prev_spec, own_spec, next_spec, ctx_spec],
        out_specs=pl.BlockSpec((tb, SWA_WIDTH), lambda i: (i, 0)),
        out_shape=jax.ShapeDtypeStruct((t, SWA_WIDTH), F32),
        compiler_params=_cparams(("arbitrary",)),
        name="swa_attention",
    )(sink, sq, sk, sk, sk, sk, sv, sv, sv, sv)


def _mix_out_kernel(x_ref, mla_ref, swa_ref, cb_ref, cv_ref, hp_ref, hn_ref, cw_ref, go_ref, wo_ref,
                    mod_ref, gf_ref, wr_ref, br_ref,
                    xo_ref, h2_ref, te_ref, gt_ref, rk_ref, cnt_ref, cnt_sc,
                    *, n_lat_tiles, n_tiles):
    i = pl.program_id(0)
    tm = x_ref.shape[0]

    @pl.when(i == 0)
    def _():
        cnt_sc[...] = jnp.zeros(cnt_sc.shape, F32)

    seg_start = (i == 0) | (i == n_lat_tiles)
    seg_end = (i == n_lat_tiles - 1) | (i == n_tiles - 1)
    v = cv_ref[...]
    rowi = lax.broadcasted_iota(jnp.int32, v.shape, 0)
    left = jnp.where(seg_start, 0.0, hp_ref[SUBLANES - 1:SUBLANES, :])
    right = jnp.where(seg_end, 0.0, hn_ref[0:1, :])
    v_dn = jnp.where(rowi == 0, left, pltpu.roll(v, 1, 0))
    v_up = jnp.where(rowi == tm - 1, right, pltpu.roll(v, tm - 1, 0))
    conv = cb_ref[...] * (v_dn * cw_ref[0:1, :] + v * cw_ref[1:2, :] + v_up * cw_ref[2:3, :])

    o1, o2 = MLA_WIDTH, MLA_WIDTH + SWA_WIDTH
    ya = (_rms(mla_ref[...]) * go_ref[:, 0:o1]).astype(BF16)
    yb = (_rms(swa_ref[...]) * go_ref[:, o1:o2]).astype(BF16)
    yc = (_rms(conv) * go_ref[:, o2:MIX_WIDTH]).astype(BF16)
    o = (jnp.dot(ya, wo_ref[0:o1, :], preferred_element_type=F32)
         + jnp.dot(yb, wo_ref[o1:o2, :], preferred_element_type=F32)
         + jnp.dot(yc, wo_ref[o2:MIX_WIDTH, :], preferred_element_type=F32))
    x = x_ref[...] + mod_ref[2:3, :] * o
    xo_ref[...] = x

    h2 = _rms(x) * gf_ref[...]
    h2 = h2 * (1.0 + mod_ref[4:5, :]) + mod_ref[3:4, :]
    h2_ref[...] = h2
    logits = jnp.dot(h2, wr_ref[...], preferred_element_type=F32, precision=lax.Precision.HIGHEST) + br_ref[...]
    e_iota = lax.broadcasted_iota(jnp.int32, logits.shape, 1)
    lane = lax.broadcasted_iota(jnp.int32, (tm, LANES), 1)
    work = logits
    tops, idxs = [], []
    for _ in range(TOP_K):
        m = jnp.max(work, axis=-1, keepdims=True)
        idx = jnp.min(jnp.where(work == m, e_iota, N_EXPERTS), axis=-1, keepdims=True)
        tops.append(m)
        idxs.append(idx)
        work = jnp.where(e_iota == idx, -jnp.inf, work)
    exps = [jnp.exp(tk_ - tops[0]) for tk_ in tops]
    denom = exps[0] + exps[1] + exps[2] + exps[3]

    onehots = [(e_iota == idx).astype(F32) for idx in idxs]
    sel = onehots[0] + onehots[1] + onehots[2] + onehots[3]
    rr = lax.broadcasted_iota(jnp.int32, (tm, tm), 0)
    cc = lax.broadcasted_iota(jnp.int32, (tm, tm), 1)
    tri = (rr > cc).astype(BF16)
    before = jnp.dot(tri, sel.astype(BF16), preferred_element_type=F32) + cnt_sc[...]
    cnt_new = cnt_sc[...] + jnp.sum(sel, axis=0, keepdims=True)
    cnt_sc[...] = cnt_new
    cnt_ref[...] = jnp.broadcast_to(cnt_new, cnt_ref.shape).astype(jnp.int32)

    te = jnp.zeros((tm, LANES), jnp.int32)
    gt = jnp.zeros((tm, LANES), F32)
    rk = jnp.zeros((tm, LANES), jnp.int32)
    for k in range(TOP_K):
        rank_k = jnp.sum(before * onehots[k], axis=-1, keepdims=True).astype(jnp.int32)
        te = jnp.where(lane == k, idxs[k], te)
        gt = jnp.where(lane == k, exps[k] / denom, gt)
        rk = jnp.where(lane == k, rank_k, rk)
    te_ref[...] = te
    gt_ref[...] = gt
    rk_ref[...] = rk


def _mix_out(xs, mla, swa, cb, cv, conv_w, g_out, w_out_b, mod, g_ffn, w_r, b_r, n_lat_tiles):
    t, d = xs.shape
    tm = ROW_TILE
    nt = t // tm
    hb = tm // SUBLANES
    row = lambda w: pl.BlockSpec((tm, w), lambda i: (i, 0))
    full = lambda a: pl.BlockSpec(a.shape, lambda i: (0,) * a.ndim)
    return pl.pallas_call(
        functools.partial(_mix_out_kernel, n_lat_tiles=n_lat_tiles, n_tiles=nt),
        grid=(nt,),
        in_specs=[row(d), row(MLA_WIDTH), row(SWA_WIDTH), row(CONV_WIDTH), row(CONV_WIDTH),
                  pl.BlockSpec((SUBLANES, CONV_WIDTH), lambda i: (jnp.maximum(i * hb - 1, 0), 0)),
                  pl.BlockSpec((SUBLANES, CONV_WIDTH), lambda i: (jnp.minimum((i + 1) * hb, nt * hb - 1), 0)),
                  full(conv_w), full(g_out), full(w_out_b),
                  pl.BlockSpec((None, N_MOD, d), lambda i: (jnp.where(i < n_lat_tiles, 0, 1), 0, 0)),
                  full(g_ffn), full(w_r), full(b_r)],
        out_specs=[row(d), row(d), row(LANES), row(LANES), row(LANES),
                   pl.BlockSpec((SUBLANES, N_EXPERTS), lambda i: (0, 0))],
        out_shape=[jax.ShapeDtypeStruct((t, d), F32), jax.ShapeDtypeStruct((t, d), F32),
                   jax.ShapeDtypeStruct((t, LANES), jnp.int32), jax.ShapeDtypeStruct((t, LANES), F32),
                   jax.ShapeDtypeStruct((t, LANES), jnp.int32),
                   jax.ShapeDtypeStruct((SUBLANES, N_EXPERTS), jnp.int32)],
        scratch_shapes=[pltpu.VMEM((1, N_EXPERTS), F32)],
        compiler_params=_cparams(("arbitrary",)),
        name="mix_out_ffn_route",
    )(xs, mla, swa, cb, cv, cv, cv, conv_w, g_out, w_out_b, mod, g_ffn, w_r, b_r)


def _dispatch_kernel(dest_ref, h_ref, zin_ref, xs_ref, sem):
    del zin_ref
    tm = h_ref.shape[0]

    def row_copy(r, k):
        return pltpu.make_async_copy(h_ref.at[pl.ds(r, 1), :],
                                     xs_ref.at[pl.ds(dest_ref[r * TOP_K + k], 1), :], sem)

    def issue(r, c):
        for k in range(TOP_K):
            row_copy(r, k).start()
        return c

    def drain(r, c):
        for k in range(TOP_K):
            row_copy(r, k).wait()
        return c

    lax.fori_loop(0, tm, issue, 0)
    lax.fori_loop(0, tm, drain, 0)


def _dispatch(dest_flat, h2, n_slots):
    t, d = h2.shape
    tm = ROW_TILE
    zeros = jnp.zeros((n_slots, d), F32)
    return pl.pallas_call(
        _dispatch_kernel,
        grid=(t // tm,),
        in_specs=[pl.BlockSpec((tm * TOP_K,), lambda i: (i,), memory_space=pltpu.SMEM),
                  pl.BlockSpec((tm, d), lambda i: (i, 0)),
                  pl.BlockSpec(memory_space=pl.ANY)],
        out_specs=pl.BlockSpec(memory_space=pl.ANY),
        out_shape=jax.ShapeDtypeStruct((n_slots, d), F32),
        scratch_shapes=[pltpu.SemaphoreType.DMA(())],
        input_output_aliases={2: 0},
        compiler_params=_cparams(("arbitrary",)),
        name="moe_dispatch",
    )(dest_flat, h2, zeros)


def _expert_kernel(be_ref, na_ref, x_ref, wg_ref, wu_ref, bg_ref, bu_ref, w2_ref, b2_ref, y_ref):
    del be_ref

    @pl.when(pl.program_id(0) < na_ref[0])
    def _():
        xb = x_ref[...].astype(BF16)
        gate = jnp.dot(xb, wg_ref[...], preferred_element_type=F32) + bg_ref[...]
        up = jnp.dot(xb, wu_ref[...], preferred_element_type=F32) + bu_ref[...]
        gate = jnp.minimum(gate, SWIGLU_LIMIT)
        up = jnp.clip(up, -SWIGLU_LIMIT, SWIGLU_LIMIT)
        act = gate * jax.nn.sigmoid(SWIGLU_ALPHA * gate) * (up + 1.0)
        y_ref[...] = jnp.dot(act.astype(BF16), w2_ref[...], preferred_element_type=F32) + b2_ref[...]


def _experts(block_expert, n_active, xs, wg, wu, bg, bu, w2, b2):
    n_slots, d = xs.shape
    bm = MOE_ROWS
    nb = n_slots // bm
    blk = lambda b, be, na: (jnp.minimum(b, na[0] - 1), 0)
    wsel = lambda b, be, na: (be[jnp.minimum(b, na[0] - 1)], 0, 0)
    grid_spec = pltpu.PrefetchScalarGridSpec(
        num_scalar_prefetch=2,
        grid=(nb,),
        in_specs=[pl.BlockSpec((bm, d), blk),
                  pl.BlockSpec((None, d, D_EXPERT), wsel),
                  pl.BlockSpec((None, d, D_EXPERT), wsel),
                  pl.BlockSpec((None, 1, D_EXPERT), wsel),
                  pl.BlockSpec((None, 1, D_EXPERT), wsel),
                  pl.BlockSpec((None, D_EXPERT, d), wsel),
                  pl.BlockSpec((None, 1, d), wsel)],
        out_specs=pl.BlockSpec((bm, d), blk),
    )
    return pl.pallas_call(
        _expert_kernel,
        grid_spec=grid_spec,
        out_shape=jax.ShapeDtypeStruct((n_slots, d), F32),
        compiler_params=_cparams(("arbitrary",)),
        name="moe_experts",
    )(block_expert, n_active, xs, wg, wu, bg, bu, w2, b2)


def _combine_kernel(dest_ref, x_ref, gt_ref, mod_ref, y_ref, o_ref, buf, sem):
    tm = x_ref.shape[0]

    def row_copy(r, k):
        return pltpu.make_async_copy(y_ref.at[pl.ds(dest_ref[r * TOP_K + k], 1), :],
                                     buf.at[k, pl.ds(r, 1), :], sem)

    def issue(r, c):
        for k in range(TOP_K):
            row_copy(r, k).start()
        return c

    def drain(r, c):
        for k in range(TOP_K):
            row_copy(r, k).wait()
        return c

    lax.fori_loop(0, tm, issue, 0)
    lax.fori_loop(0, tm, drain, 0)
    gt = gt_ref[...]
    f = gt[:, 0:1] * buf[0]
    for k in range(1, TOP_K):
        f = f + gt[:, k:k + 1] * buf[k]
    o_ref[...] = x_ref[...] + mod_ref[5:6, :] * f


def _combine(dest_flat, xs, gates, mod, y, n_lat_tiles):
    t, d = xs.shape
    tm = ROW_TILE
    return pl.pallas_call(
        _combine_kernel,
        grid=(t // tm,),
        in_specs=[pl.BlockSpec((tm * TOP_K,), lambda i: (i,), memory_space=pltpu.SMEM),
                  pl.BlockSpec((tm, d), lambda i: (i, 0)),
                  pl.BlockSpec((tm, LANES), lambda i: (i, 0)),
                  pl.BlockSpec((None, N_MOD, d), lambda i: (jnp.where(i < n_lat_tiles, 0, 1), 0, 0)),
                  pl.BlockSpec(memory_space=pl.ANY)],
        out_specs=pl.BlockSpec((tm, d), lambda i: (i, 0)),
        out_shape=jax.ShapeDtypeStruct((t, d), F32),
        scratch_shapes=[pltpu.VMEM((TOP_K, tm, d), F32), pltpu.SemaphoreType.DMA(())],
        compiler_params=_cparams(("arbitrary",)),
        name="moe_combine",
    )(dest_flat, xs, gates, mod, y)


def _final_norm_kernel(x_ref, g_ref, o_ref):
    o_ref[...] = _rms(x_ref[...]) * g_ref[...]


def _final_norm(xs, g, n_lat):
    d = xs.shape[1]
    tm = ROW_TILE
    return pl.pallas_call(
        _final_norm_kernel,
        grid=(n_lat // tm,),
        in_specs=[pl.BlockSpec((tm, d), lambda i: (i, 0)), pl.BlockSpec((1, d), lambda i: (0, 0))],
        out_specs=pl.BlockSpec((tm, d), lambda i: (i, 0)),
        out_shape=jax.ShapeDtypeStruct((n_lat, d), F32),
        compiler_params=_cparams(("arbitrary",)),
        name="final_norm",
    )(xs, g)


def _rope_tables(n_lat, n_ctx):
    rows = n_lat // GRID_W
    row_id, col_id = jnp.meshgrid(jnp.arange(rows), jnp.arange(GRID_W), indexing="ij")
    half = MLA_ROPE // 2
    inv_freq = ROPE_THETA ** (-jnp.arange(0, half, 2, dtype=F32) / half)

    def axis_angles(pos):
        a = pos.reshape(-1).astype(F32)[:, None] * inv_freq[None, :]
        return jnp.concatenate([a, a], axis=-1)

    ang = jnp.concatenate([axis_angles(row_id), axis_angles(col_id)], axis=-1)
    cos, sin = jnp.cos(ang), jnp.sin(ang)
    sign = jnp.where((jnp.arange(MLA_ROPE) % 32) < 16, -1.0, 1.0).astype(F32)
    sin = sin * sign[None, :]
    cos = jnp.concatenate([cos, jnp.ones((n_ctx, MLA_ROPE), F32)], axis=0)
    sin = jnp.concatenate([sin, jnp.zeros((n_ctx, MLA_ROPE), F32)], axis=0)
    return jnp.concatenate([cos, cos], axis=1), jnp.concatenate([sin, sin], axis=1)


def _split_in_proj(w):
    d = w.shape[0]
    sizes = (MLA_Q_RANK, MLA_KV_RANK, MLA_ROPE, SWA_WIDTH, SWA_KV_WIDTH, SWA_KV_WIDTH,
             CONV_WIDTH, CONV_WIDTH, CONV_WIDTH)
    offs = np.concatenate([[0], np.cumsum(sizes)])
    part = [w[:, offs[j]:offs[j + 1]] for j in range(len(sizes))]
    cq, ckv, kr, sq, sk, sv, cb, cc, cx = part
    return jnp.concatenate([cq, ckv, sq, sk, sv, cb, cc, cx, kr, jnp.zeros((d, LANES - MLA_ROPE), w.dtype)],
                           axis=1).astype(BF16)


def _split_uq(w):
    r = w.shape[0]
    w = w.reshape(r, MLA_HEADS, MLA_NOPE + MLA_ROPE)
    wn = w[:, :, :MLA_NOPE].reshape(r, MLA_HEADS * MLA_NOPE)
    wr = jnp.concatenate([w[:, :, MLA_NOPE:], jnp.zeros((r, MLA_HEADS, LANES - MLA_ROPE), w.dtype)], axis=2)
    return wn.astype(BF16), wr.reshape(r, MLA_HEADS * LANES).astype(BF16)


def kernel(x, c, ctx, c_ctx, w_ada, b_ada, g_mix, w_in, g_mla_q, g_mla_kv, w_mla_uq, w_mla_ukv,
           swa_sink, conv_w, g_out, w_out, g_ffn, w_router, b_router, w_exp1, b_exp1, w_exp2,
           b_exp2, g_final):
    bsz, n_lat, d = x.shape
    n_ctx = ctx.shape[1]
    depth = w_ada.shape[0]
    assert bsz == 1 and n_lat % (4 * ROW_TILE) == 0 and n_ctx == ROW_TILE and d % LANES == 0
    t = n_lat + n_ctx
    n_lat_tiles = n_lat // ROW_TILE

    xs = jnp.concatenate([x[0], ctx[0]], axis=0)
    cos, sin = _rope_tables(n_lat, n_ctx)
    mods = _ada(c, c_ctx, w_ada, b_ada).reshape(depth, 2, N_MOD, d)

    n_blocks = -(-t * TOP_K // MOE_ROWS) + N_EXPERTS
    n_slots = n_blocks * MOE_ROWS

    for l in range(depth):
        mod = mods[l]
        cq, ckv, kr, sq, sk, sv, cb, cv = _inproj(xs, g_mix[l][None], mod, cos, sin,
                                                  _split_in_proj(w_in[l]), n_lat_tiles)
        wqn, wqr = _split_uq(w_mla_uq[l])
        q, k, v = _mla_up(cq, ckv, kr, g_mla_q[l][None], g_mla_kv[l][None], cos, sin,
                          wqn, wqr, w_mla_ukv[l].astype(BF16))
        mla = _mla_attention(q, k, v, n_lat, n_ctx)
        swa = _swa(swa_sink[l], sq, sk, sv, n_lat, n_ctx)
        xs, h2, top_e, gates, rank, counts = _mix_out(
            xs, mla, swa, cb, cv, conv_w[l], g_out[l][None], w_out[l].astype(BF16), mod,
            g_ffn[l][None], w_router[l], b_router[l][None], n_lat_tiles)

        cnt = counts[0]
        padded = (cnt + MOE_ROWS - 1) // MOE_ROWS * MOE_ROWS
        pend = jnp.cumsum(padded)
        pstart = pend - padded
        dest = (pstart[top_e[:, :TOP_K]] + rank[:, :TOP_K]).reshape(-1).astype(jnp.int32)
        bstart = jnp.arange(n_blocks, dtype=jnp.int32) * MOE_ROWS
        block_expert = jnp.minimum(jnp.searchsorted(pend, bstart, side="right"), N_EXPERTS - 1).astype(jnp.int32)
        n_active = (pend[-1:] // MOE_ROWS).astype(jnp.int32)

        xg = _dispatch(dest, h2, n_slots)
        w1 = w_exp1[l]
        y = _experts(block_expert, n_active, xg,
                     w1[:, :, 0::2].astype(BF16), w1[:, :, 1::2].astype(BF16),
                     b_exp1[l][:, None, 0::2], b_exp1[l][:, None, 1::2],
                     w_exp2[l].astype(BF16), b_exp2[l][:, None, :])
        xs = _combine(dest, xs, gates, mod, y, n_lat_tiles)

    return _final_norm(xs, g_final[None], n_lat).reshape(1, n_lat, d)
```

```python
import functools

import numpy as np
import jax
import jax.numpy as jnp
from jax import lax
from jax.experimental import pallas as pl
from jax.experimental.pallas import tpu as pltpu

F32 = jnp.float32
BF16 = jnp.bfloat16

GRID_W = 64
ROPE_THETA = 10000.0
NORM_EPS = 1e-6
N_MOD = 6
MLA_HEADS = 8
MLA_Q_RANK = 512
MLA_KV_RANK = 256
MLA_NOPE = 128
MLA_ROPE = 64
MLA_V = 128
MLA_WIDTH = MLA_HEADS * MLA_V
MLA_SCALE = (MLA_NOPE + MLA_ROPE) ** -0.5
SWA_HEADS = 8
SWA_KV_HEADS = 2
SWA_GROUP = SWA_HEADS // SWA_KV_HEADS
SWA_HEAD_DIM = 64
SWA_WINDOW = 128
SWA_WIDTH = SWA_HEADS * SWA_HEAD_DIM
SWA_KV_WIDTH = SWA_KV_HEADS * SWA_HEAD_DIM
SWA_SCALE = SWA_HEAD_DIM ** -0.5
CONV_WIDTH = 512
CONV_K = 3
MIX_WIDTH = MLA_WIDTH + SWA_WIDTH + CONV_WIDTH
N_EXPERTS = 32
TOP_K = 4
D_EXPERT = 512
SWIGLU_LIMIT = 7.0
SWIGLU_ALPHA = 1.702
LOG2E = 1.4426950408889634

LANES = 128
SUBLANES = 8
VMEM_LIMIT = 56 * 1024 * 1024

ROW_TILE = 256
SWA_TILE = 128
MOE_ROWS = 256
ATTN_Q_TILE = 1024
ATTN_K_CHUNK = 512
ATTN_UNROLL = 4
PAD_KEY_SCORE = -1e30

_O_CQ = 0
_O_CKV = _O_CQ + MLA_Q_RANK
_O_SQ = _O_CKV + MLA_KV_RANK
_O_SK = _O_SQ + SWA_WIDTH
_O_SV = _O_SK + SWA_KV_WIDTH
_O_CB = _O_SV + SWA_KV_WIDTH
_O_CC = _O_CB + CONV_WIDTH
_O_CX = _O_CC + CONV_WIDTH
_O_KR = _O_CX + CONV_WIDTH
N_IN_PAD = _O_KR + LANES


def _cparams(sem):
    return pltpu.CompilerParams(dimension_semantics=sem, vmem_limit_bytes=VMEM_LIMIT)


def _rms(x):
    return x * lax.rsqrt(jnp.mean(x * x, axis=-1, keepdims=True) + NORM_EPS)


def _rope(u, cos, sin_signed):
    w = u.shape[-1]
    reps = w // LANES
    if reps > 1:
        cos = jnp.concatenate([cos] * reps, axis=1)
        sin_signed = jnp.concatenate([sin_signed] * reps, axis=1)
    lane = lax.broadcasted_iota(jnp.int32, u.shape, 1)
    first = (lane % 32) < 16
    rot = jnp.where(first, pltpu.roll(u, w - 16, 1), pltpu.roll(u, 16, 1))
    return u * cos + rot * sin_signed


def _ada_kernel(s_ref, w_ref, b_ref, o_ref, *, chunk):
    d, tn = w_ref.shape

    def body(i, accs):
        a0, a1 = accs
        r0 = pl.multiple_of(i * chunk, chunk)
        w = w_ref[pl.ds(r0, chunk), :]
        s = s_ref[pl.ds(r0, chunk), :]
        s = s * jax.nn.sigmoid(s)
        p0 = (w * s[:, 0:1]).reshape(chunk // SUBLANES, SUBLANES, tn).sum(axis=0)
        p1 = (w * s[:, 1:2]).reshape(chunk // SUBLANES, SUBLANES, tn).sum(axis=0)
        return a0 + p0, a1 + p1

    z = jnp.zeros((SUBLANES, tn), F32)
    a0, a1 = lax.fori_loop(0, d // chunk, body, (z, z))
    b = b_ref[...]
    o_ref[0:1, :] = jnp.sum(a0, axis=0, keepdims=True) + b
    o_ref[1:2, :] = jnp.sum(a1, axis=0, keepdims=True) + b


def _ada(c, c_ctx, w_ada, b_ada):
    depth, d, n = w_ada.shape
    tn = 1024 if n % 1024 == 0 else 512
    chunk = 64
    s = jnp.stack([c.reshape(d), c_ctx.reshape(d)], axis=1)
    return pl.pallas_call(
        functools.partial(_ada_kernel, chunk=chunk),
        grid=(depth, n // tn),
        in_specs=[
            pl.BlockSpec((d, 2), lambda l, j: (0, 0)),
            pl.BlockSpec((None, d, tn), lambda l, j: (l, 0, j)),
            pl.BlockSpec((None, 1, tn), lambda l, j: (l, 0, j)),
        ],
        out_specs=pl.BlockSpec((None, 2, tn), lambda l, j: (l, 0, j)),
        out_shape=jax.ShapeDtypeStruct((depth, 2, n), F32),
        compiler_params=_cparams(("arbitrary", "arbitrary")),
        name="ada_mod",
    )(s, w_ada, b_ada.reshape(depth, 1, n))


def _inproj_kernel(x_ref, g_ref, mod_ref, cos_ref, sin_ref, w_ref,
                   cq_ref, ckv_ref, kr_ref, sq_ref, sk_ref, sv_ref, cb_ref, cv_ref):
    x = x_ref[...]
    h = _rms(x) * g_ref[...]
    h = h * (1.0 + mod_ref[1:2, :]) + mod_ref[0:1, :]
    hb = h.astype(BF16)
    cos = cos_ref[...]
    sin = sin_ref[...]

    def proj(a, width):
        return jnp.dot(hb, w_ref[:, a:a + width], preferred_element_type=F32)

    cq_ref[...] = proj(_O_CQ, MLA_Q_RANK)
    ckv_ref[...] = proj(_O_CKV, MLA_KV_RANK)
    kr_ref[...] = _rope(proj(_O_KR, LANES), cos, sin).astype(BF16)
    sq_ref[...] = (_rope(proj(_O_SQ, SWA_WIDTH), cos, sin) * SWA_SCALE).astype(BF16)
    sk_ref[...] = _rope(proj(_O_SK, SWA_KV_WIDTH), cos, sin).astype(BF16)
    sv_ref[...] = proj(_O_SV, SWA_KV_WIDTH).astype(BF16)
    cb_ref[...] = proj(_O_CB, CONV_WIDTH)
    cv_ref[...] = proj(_O_CC, CONV_WIDTH) * proj(_O_CX, CONV_WIDTH)


def _inproj(xs, g, mod, cos, sin, w_in_p, n_lat_tiles):
    t, d = xs.shape
    tm = ROW_TILE
    row = lambda w: pl.BlockSpec((tm, w), lambda i: (i, 0))
    outs = [
        (MLA_Q_RANK, F32), (MLA_KV_RANK, F32), (LANES, BF16), (SWA_WIDTH, BF16),
        (SWA_KV_WIDTH, BF16), (SWA_KV_WIDTH, BF16), (CONV_WIDTH, F32), (CONV_WIDTH, F32),
    ]
    return pl.pallas_call(
        _inproj_kernel,
        grid=(t // tm,),
        in_specs=[
            row(d),
            pl.BlockSpec((1, d), lambda i: (0, 0)),
            pl.BlockSpec((None, N_MOD, d), lambda i: (jnp.where(i < n_lat_tiles, 0, 1), 0, 0)),
            row(LANES), row(LANES),
            pl.BlockSpec((d, N_IN_PAD), lambda i: (0, 0)),
        ],
        out_specs=[row(w) for w, _ in outs],
        out_shape=[jax.ShapeDtypeStruct((t, w), dt) for w, dt in outs],
        compiler_params=_cparams(("arbitrary",)),
        name="mixer_in_proj",
    )(xs, g, mod, cos, sin, w_in_p)


def _mla_up_kernel(cq_ref, ckv_ref, kr_ref, gq_ref, gkv_ref, cos_ref, sin_ref,
                   wqn_ref, wqr_ref, wkv_ref, q_ref, k_ref, v_ref, *, n_real_tiles):
    i = pl.program_id(0)
    tm = cq_ref.shape[0]
    last_lane = lax.broadcasted_iota(jnp.int32, (tm, LANES), 1) == LANES - 1

    @pl.when(i < n_real_tiles)
    def _():
        hq = (_rms(cq_ref[...]) * gq_ref[...]).astype(BF16)
        qs = MLA_SCALE * LOG2E
        qn = jnp.dot(hq, wqn_ref[...], preferred_element_type=F32) * qs
        qr = _rope(jnp.dot(hq, wqr_ref[...], preferred_element_type=F32), cos_ref[...], sin_ref[...]) * qs
        hk = (_rms(ckv_ref[...]) * gkv_ref[...]).astype(BF16)
        kv = jnp.dot(hk, wkv_ref[...], preferred_element_type=F32)
        kr = kr_ref[...]
        ones = jnp.ones((tm, LANES), BF16)
        for h in range(MLA_HEADS):
            q_ref[h, :, 0:LANES] = qn[:, h * LANES:(h + 1) * LANES].astype(BF16)
            q_ref[h, :, LANES:2 * LANES] = jnp.where(last_lane, 1.0, qr[:, h * LANES:(h + 1) * LANES]).astype(BF16)
            k_ref[h, :, 0:LANES] = kv[:, h * 2 * LANES:h * 2 * LANES + LANES].astype(BF16)
            k_ref[h, :, LANES:2 * LANES] = kr
            v_ref[h, :, 0:LANES] = kv[:, h * 2 * LANES + LANES:(h + 1) * 2 * LANES].astype(BF16)
            v_ref[h, :, LANES:2 * LANES] = ones

    @pl.when(i >= n_real_tiles)
    def _():
        q_ref[...] = jnp.zeros(q_ref.shape, BF16)
        v_ref[...] = jnp.zeros(v_ref.shape, BF16)
        zero = jnp.zeros((tm, LANES), BF16)
        flag = jnp.where(last_lane, PAD_KEY_SCORE, 0.0).astype(BF16)
        for h in range(MLA_HEADS):
            k_ref[h, :, 0:LANES] = zero
            k_ref[h, :, LANES:2 * LANES] = flag


def _mla_up(cq, ckv, kr, gq, gkv, cos, sin, wqn, wqr, wkv, t_pad):
    t = cq.shape[0]
    tm = ROW_TILE
    nr = t // tm
    row = lambda w: pl.BlockSpec((tm, w), lambda i: (jnp.minimum(i, nr - 1), 0))
    full = lambda a: pl.BlockSpec(a.shape, lambda i: (0,) * a.ndim)
    hd = pl.BlockSpec((MLA_HEADS, tm, 2 * LANES), lambda i: (0, i, 0))
    shp = jax.ShapeDtypeStruct((MLA_HEADS, t_pad, 2 * LANES), BF16)
    return pl.pallas_call(
        functools.partial(_mla_up_kernel, n_real_tiles=nr),
        grid=(t_pad // tm,),
        in_specs=[row(MLA_Q_RANK), row(MLA_KV_RANK), row(LANES), full(gq), full(gkv),
                  row(LANES), row(LANES), full(wqn), full(wqr), full(wkv)],
        out_specs=[hd, hd, hd],
        out_shape=[shp, shp, shp],
        compiler_params=_cparams(("arbitrary",)),
        name="mla_up_proj",
    )(cq, ckv, kr, gq, gkv, cos, sin, wqn, wqr, wkv)


def _mla_attn_kernel(q_ref, k_ref, v_ref, o_ref, s_buf, m_sc, acc_sc, *, n_chunks, ch, unroll):
    reps = ch // LANES
    m_sc[...] = jnp.full(m_sc.shape, -jnp.inf, F32)
    acc_sc[...] = jnp.zeros(acc_sc.shape, F32)

    def scores(c, slot):
        kc = k_ref[pl.ds(pl.multiple_of(c * ch, ch), ch), :]
        s_buf[slot] = lax.dot_general(q_ref[...], kc, (((1,), (1,)), ((), ())), preferred_element_type=F32)

    def softmax_pv(c, slot):
        s = s_buf[slot]
        m_prev = m_sc[...]
        m_new = jnp.maximum(m_prev, jnp.max(s, axis=-1, keepdims=True))
        alpha = jnp.exp2(m_prev - m_new)
        p = jnp.exp2(s - jnp.concatenate([m_new] * reps, axis=1)).astype(BF16)
        vc = v_ref[pl.ds(pl.multiple_of(c * ch, ch), ch), :]
        pv = jnp.dot(p, vc, preferred_element_type=F32)
        acc_sc[...] = acc_sc[...] * jnp.concatenate([alpha, alpha], axis=1) + pv
        m_sc[...] = m_new

    scores(0, 0)

    def trip(j, carry):
        for u in range(unroll):
            scores(j * unroll + u + 1, (u + 1) % 2)
            softmax_pv(j * unroll + u, u % 2)
        return carry

    n_trips = (n_chunks - 1) // unroll
    lax.fori_loop(0, n_trips, trip, 0)
    for c in range(n_trips * unroll, n_chunks):
        if c + 1 < n_chunks:
            scores(c + 1, (c + 1) % 2)
        softmax_pv(c, c % 2)
    acc = acc_sc[...]
    o_ref[...] = acc[:, 0:MLA_V] / acc[:, MLA_V:2 * MLA_V]


def _mla_ctx_kernel(q_ref, k_ref, v_ref, prev_ref, o_ref):
    del prev_ref
    s = lax.dot_general(q_ref[...], k_ref[...], (((1,), (1,)), ((), ())), preferred_element_type=F32)
    p = jnp.exp2(s - jnp.max(s, axis=-1, keepdims=True))
    pv = jnp.dot(p.astype(BF16), v_ref[...], preferred_element_type=F32)
    o_ref[...] = pv[:, 0:MLA_V] / pv[:, MLA_V:2 * MLA_V]


def _mla_attention(q, k, v, n_lat, n_ctx):
    t = n_lat + n_ctx
    t_pad = k.shape[1]
    tq = ATTN_Q_TILE if n_lat % ATTN_Q_TILE == 0 else ROW_TILE
    ch = ATTN_K_CHUNK
    wide = 2 * LANES
    out = pl.pallas_call(
        functools.partial(_mla_attn_kernel, n_chunks=t_pad // ch, ch=ch, unroll=ATTN_UNROLL),
        grid=(MLA_HEADS, n_lat // tq),
        in_specs=[pl.BlockSpec((None, tq, wide), lambda h, i: (h, i, 0)),
                  pl.BlockSpec((None, t_pad, wide), lambda h, i: (h, 0, 0)),
                  pl.BlockSpec((None, t_pad, wide), lambda h, i: (h, 0, 0))],
        out_specs=pl.BlockSpec((tq, MLA_V), lambda h, i: (i, h)),
        out_shape=jax.ShapeDtypeStruct((t, MLA_WIDTH), F32),
        scratch_shapes=[pltpu.VMEM((2, tq, ch), F32), pltpu.VMEM((tq, LANES), F32), pltpu.VMEM((tq, wide), F32)],
        compiler_params=_cparams(("arbitrary", "arbitrary")),
        name="mla_attention",
    )(q, k, v)
    cb = n_lat // n_ctx
    return pl.pallas_call(
        _mla_ctx_kernel,
        grid=(MLA_HEADS,),
        in_specs=[pl.BlockSpec((None, n_ctx, wide), lambda h: (h, cb, 0)),
                  pl.BlockSpec((None, n_ctx, wide), lambda h: (h, cb, 0)),
                  pl.BlockSpec((None, n_ctx, wide), lambda h: (h, cb, 0)),
                  pl.BlockSpec(memory_space=pl.ANY)],
        out_specs=pl.BlockSpec((n_ctx, MLA_V), lambda h: (cb, h)),
        out_shape=jax.ShapeDtypeStruct((t, MLA_WIDTH), F32),
        input_output_aliases={3: 0},
        compiler_params=_cparams(("arbitrary",)),
        name="mla_attention_ctx",
    )(q, k, v, out)


def _swa_kernel(sink_ref, q_ref, kp_ref, ko_ref, kn_ref, kc_ref, vp_ref, vo_ref, vn_ref, vc_ref, o_ref,
                *, n_lat_tiles, n_lat, n_ctx):
    i = pl.program_id(0)
    tb = SWA_TILE
    nk = n_ctx + 3 * tb
    kall = jnp.concatenate([kc_ref[...], kp_ref[...], ko_ref[...], kn_ref[...]], axis=0)
    vall = jnp.concatenate([vc_ref[...], vp_ref[...], vo_ref[...], vn_ref[...]], axis=0)
    col = lax.broadcasted_iota(jnp.int32, (tb, nk), 1)
    r = lax.broadcasted_iota(jnp.int32, (tb, nk), 0)
    j = col - n_ctx
    kpos = (i - 1) * tb + j
    local_ok = (jnp.abs(j - tb - r) <= SWA_WINDOW) & (kpos >= 0) & (kpos < n_lat) & (i < n_lat_tiles)
    valid = (col < n_ctx) | local_ok
    lane_kv = lax.broadcasted_iota(jnp.int32, (nk, LANES), 1)
    k_roll = pltpu.roll(kall, SWA_HEAD_DIM, 1)
    v_roll = pltpu.roll(vall, SWA_HEAD_DIM, 1)
    gw = SWA_GROUP * SWA_HEAD_DIM
    lane_g = lax.broadcasted_iota(jnp.int32, (tb, gw), 1) // SWA_HEAD_DIM
    lane_vg = lax.broadcasted_iota(jnp.int32, (nk, gw), 1) // SWA_HEAD_DIM
    for kvh in range(SWA_KV_HEADS):
        lo = lane_kv < SWA_HEAD_DIM
        if kvh == 0:
            k2 = jnp.where(lo, kall, k_roll)
            v2 = jnp.where(lo, vall, v_roll)
        else:
            k2 = jnp.where(lo, k_roll, kall)
            v2 = jnp.where(lo, v_roll, vall)
        kt = jnp.concatenate([k2, k2], axis=1)
        vt = jnp.concatenate([v2, v2], axis=1)
        qg = q_ref[:, kvh * gw:(kvh + 1) * gw]
        acc = jnp.zeros((tb, gw), F32)
        for g in range(SWA_GROUP):
            qm = jnp.where(lane_g == g, qg, jnp.zeros_like(qg))
            s = lax.dot_general(qm, kt, (((1,), (1,)), ((), ())), preferred_element_type=F32)
            s = jnp.where(valid, s, -jnp.inf)
            sink = sink_ref[kvh * SWA_GROUP + g]
            m = jnp.maximum(jnp.max(s, axis=-1, keepdims=True), sink)
            p = jnp.exp(s - m)
            denom = jnp.sum(p, axis=-1, keepdims=True) + jnp.exp(sink - m)
            p = (p / denom).astype(BF16)
            vm = jnp.where(lane_vg == g, vt, jnp.zeros_like(vt))
            acc = acc + jnp.dot(p, vm, preferred_element_type=F32)
        o_ref[:, kvh * gw:(kvh + 1) * gw] = acc


def _swa(sink, sq, sk, sv, n_lat, n_ctx):
    t = n_lat + n_ctx
    tb = SWA_TILE
    nt = t // tb
    nlt = n_lat // tb
    kvw = SWA_KV_WIDTH
    prev_spec = pl.BlockSpec((tb, kvw), lambda i: (jnp.maximum(i - 1, 0), 0))
    own_spec = pl.BlockSpec((tb, kvw), lambda i: (i, 0))
    next_spec = pl.BlockSpec((tb, kvw), lambda i: (jnp.minimum(i + 1, nt - 1), 0))
    ctx_spec = pl.BlockSpec((n_ctx, kvw), lambda i: (n_lat // n_ctx, 0))
    return pl.pallas_call(
        functools.partial(_swa_kernel, n_lat_tiles=nlt, n_lat=n_lat, n_ctx=n_ctx),
        grid=(nt,),
        in_specs=[pl.BlockSpec(memory_space=pltpu.SMEM),
                  pl.BlockSpec((tb, SWA_WIDTH), lambda i: (i, 0)),
                  prev_spec, own_spec, next_spec, ctx_spec,
                  prev_spec, own_spec, next_spec, ctx_spec],
        out_specs=pl.BlockSpec((tb, SWA_WIDTH), lambda i: (i, 0)),
        out_shape=jax.ShapeDtypeStruct((t, SWA_WIDTH), F32),
        compiler_params=_cparams(("arbitrary",)),
        name="swa_attention",
    )(sink, sq, sk, sk, sk, sk, sv, sv, sv, sv)


def _mix_out_kernel(x_ref, mla_ref, swa_ref, cb_ref, cv_ref, hp_ref, hn_ref, cw_ref, go_ref, wo_ref,
                    mod_ref, gf_ref, wr_ref, br_ref,
                    xo_ref, h2_ref, te_ref, gt_ref, rk_ref, cnt_ref, cnt_sc,
                    *, n_lat_tiles, n_tiles):
    i = pl.program_id(0)
    tm = x_ref.shape[0]

    @pl.when(i == 0)
    def _():
        cnt_sc[...] = jnp.zeros(cnt_sc.shape, F32)

    seg_start = (i == 0) | (i == n_lat_tiles)
    seg_end = (i == n_lat_tiles - 1) | (i == n_tiles - 1)
    v = cv_ref[...]
    rowi = lax.broadcasted_iota(jnp.int32, v.shape, 0)
    left = jnp.where(seg_start, 0.0, hp_ref[SUBLANES - 1:SUBLANES, :])
    right = jnp.where(seg_end, 0.0, hn_ref[0:1, :])
    v_dn = jnp.where(rowi == 0, left, pltpu.roll(v, 1, 0))
    v_up = jnp.where(rowi == tm - 1, right, pltpu.roll(v, tm - 1, 0))
    conv = cb_ref[...] * (v_dn * cw_ref[0:1, :] + v * cw_ref[1:2, :] + v_up * cw_ref[2:3, :])

    o1, o2 = MLA_WIDTH, MLA_WIDTH + SWA_WIDTH
    ya = (_rms(mla_ref[...]) * go_ref[:, 0:o1]).astype(BF16)
    yb = (_rms(swa_ref[...]) * go_ref[:, o1:o2]).astype(BF16)
    yc = (_rms(conv) * go_ref[:, o2:MIX_WIDTH]).astype(BF16)
    o = (jnp.dot(ya, wo_ref[0:o1, :], preferred_element_type=F32)
         + jnp.dot(yb, wo_ref[o1:o2, :], preferred_element_type=F32)
         + jnp.dot(yc, wo_ref[o2:MIX_WIDTH, :], preferred_element_type=F32))
    x = x_ref[...] + mod_ref[2:3, :] * o
    xo_ref[...] = x

    h2 = _rms(x) * gf_ref[...]
    h2 = h2 * (1.0 + mod_ref[4:5, :]) + mod_ref[3:4, :]
    h2_ref[...] = h2
    h_hi = h2.astype(BF16)
    h_lo = (h2 - h_hi.astype(F32)).astype(BF16)
    parts = (jnp.dot(h_hi, wr_ref[...], preferred_element_type=F32)
             + jnp.dot(h_lo, wr_ref[...], preferred_element_type=F32))
    logits = parts[:, 0:N_EXPERTS] + parts[:, N_EXPERTS:2 * N_EXPERTS] + br_ref[...]
    e_iota = lax.broadcasted_iota(jnp.int32, logits.shape, 1)
    lane = lax.broadcasted_iota(jnp.int32, (tm, LANES), 1)
    work = logits
    tops, idxs = [], []
    for _ in range(TOP_K):
        m = jnp.max(work, axis=-1, keepdims=True)
        idx = jnp.min(jnp.where(work == m, e_iota, N_EXPERTS), axis=-1, keepdims=True)
        tops.append(m)
        idxs.append(idx)
        work = jnp.where(e_iota == idx, -jnp.inf, work)
    exps = [jnp.exp(tk_ - tops[0]) for tk_ in tops]
    denom = exps[0] + exps[1] + exps[2] + exps[3]

    onehots = [(e_iota == idx).astype(F32) for idx in idxs]
    sel = onehots[0] + onehots[1] + onehots[2] + onehots[3]
    rr = lax.broadcasted_iota(jnp.int32, (tm, tm), 0)
    cc = lax.broadcasted_iota(jnp.int32, (tm, tm), 1)
    tri = (rr > cc).astype(BF16)
    before = jnp.dot(tri, sel.astype(BF16), preferred_element_type=F32) + cnt_sc[...]
    cnt_new = cnt_sc[...] + jnp.sum(sel, axis=0, keepdims=True)
    cnt_sc[...] = cnt_new
    cnt_ref[...] = jnp.broadcast_to(cnt_new, cnt_ref.shape).astype(jnp.int32)

    te = jnp.zeros((tm, LANES), jnp.int32)
    gt = jnp.zeros((tm, LANES), F32)
    rk = jnp.zeros((tm, LANES), jnp.int32)
    for k in range(TOP_K):
        rank_k = jnp.sum(before * onehots[k], axis=-1, keepdims=True).astype(jnp.int32)
        te = jnp.where(lane == k, idxs[k], te)
        gt = jnp.where(lane == k, exps[k] / denom, gt)
        rk = jnp.where(lane == k, rank_k, rk)
    te_ref[...] = te
    gt_ref[...] = gt
    rk_ref[...] = rk


def _mix_out(xs, mla, swa, cb, cv, conv_w, g_out, w_out_b, mod, g_ffn, w_r, b_r, n_lat_tiles):
    t, d = xs.shape
    tm = ROW_TILE
    nt = t // tm
    hb = tm // SUBLANES
    row = lambda w: pl.BlockSpec((tm, w), lambda i: (i, 0))
    full = lambda a: pl.BlockSpec(a.shape, lambda i: (0,) * a.ndim)
    return pl.pallas_call(
        functools.partial(_mix_out_kernel, n_lat_tiles=n_lat_tiles, n_tiles=nt),
        grid=(nt,),
        in_specs=[row(d), row(MLA_WIDTH), row(SWA_WIDTH), row(CONV_WIDTH), row(CONV_WIDTH),
                  pl.BlockSpec((SUBLANES, CONV_WIDTH), lambda i: (jnp.maximum(i * hb - 1, 0), 0)),
                  pl.BlockSpec((SUBLANES, CONV_WIDTH), lambda i: (jnp.minimum((i + 1) * hb, nt * hb - 1), 0)),
                  full(conv_w), full(g_out), full(w_out_b),
                  pl.BlockSpec((None, N_MOD, d), lambda i: (jnp.where(i < n_lat_tiles, 0, 1), 0, 0)),
                  full(g_ffn), full(w_r), full(b_r)],
        out_specs=[row(d), row(d), row(LANES), row(LANES), row(LANES),
                   pl.BlockSpec((SUBLANES, N_EXPERTS), lambda i: (0, 0))],
        out_shape=[jax.ShapeDtypeStruct((t, d), F32), jax.ShapeDtypeStruct((t, d), F32),
                   jax.ShapeDtypeStruct((t, LANES), jnp.int32), jax.ShapeDtypeStruct((t, LANES), F32),
                   jax.ShapeDtypeStruct((t, LANES), jnp.int32),
                   jax.ShapeDtypeStruct((SUBLANES, N_EXPERTS), jnp.int32)],
        scratch_shapes=[pltpu.VMEM((1, N_EXPERTS), F32)],
        compiler_params=_cparams(("arbitrary",)),
        name="mix_out_ffn_route",
    )(xs, mla, swa, cb, cv, cv, cv, conv_w, g_out, w_out_b, mod, g_ffn, w_r, b_r)


def _dispatch_kernel(pstart_ref, pend_ref, te_ref, rk_ref, h_ref, xs_ref, zero_buf, sem, zsem):
    tm = h_ref.shape[0]
    bm = zero_buf.shape[0]

    @pl.when(pl.program_id(0) == 0)
    def _():
        zero_buf[...] = jnp.zeros(zero_buf.shape, F32)

        def tail_copy(e):
            last = pl.multiple_of(jnp.maximum(pend_ref[e] - bm, 0), bm)
            return pltpu.make_async_copy(zero_buf, xs_ref.at[pl.ds(last, bm), :], zsem)

        def zstart(e, c):
            @pl.when(pend_ref[e] > pstart_ref[e])
            def _():
                tail_copy(e).start()
            return c

        def zwait(e, c):
            @pl.when(pend_ref[e] > pstart_ref[e])
            def _():
                tail_copy(e).wait()
            return c

        lax.fori_loop(0, N_EXPERTS, zstart, 0)
        lax.fori_loop(0, N_EXPERTS, zwait, 0)

    def row_copy(r, k):
        a = r * TOP_K + k
        dest = pstart_ref[te_ref[a]] + rk_ref[a]
        return pltpu.make_async_copy(h_ref.at[pl.ds(r, 1), :], xs_ref.at[pl.ds(dest, 1), :], sem)

    def issue(r, c):
        for k in range(TOP_K):
            row_copy(r, k).start()
        return c

    def drain(r, c):
        for k in range(TOP_K):
            row_copy(r, k).wait()
        return c

    lax.fori_loop(0, tm, issue, 0)
    lax.fori_loop(0, tm, drain, 0)


def _dispatch(pstart, pend, te_flat, rk_flat, h2, n_slots):
    t, d = h2.shape
    tm = ROW_TILE
    flat = pl.BlockSpec((tm * TOP_K,), lambda i, ps, pe: (i,), memory_space=pltpu.SMEM)
    grid_spec = pltpu.PrefetchScalarGridSpec(
        num_scalar_prefetch=2,
        grid=(t // tm,),
        in_specs=[flat, flat, pl.BlockSpec((tm, d), lambda i, ps, pe: (i, 0))],
        out_specs=pl.BlockSpec(memory_space=pl.ANY),
        scratch_shapes=[pltpu.VMEM((MOE_ROWS, d), F32), pltpu.SemaphoreType.DMA(()), pltpu.SemaphoreType.DMA(())],
    )
    return pl.pallas_call(
        _dispatch_kernel,
        grid_spec=grid_spec,
        out_shape=jax.ShapeDtypeStruct((n_slots, d), F32),
        compiler_params=_cparams(("arbitrary",)),
        name="moe_dispatch",
    )(pstart, pend, te_flat, rk_flat, h2)


def _expert_kernel(be_ref, na_ref, x_ref, w1_ref, b1_ref, w2_ref, b2_ref, sel_ref, y_ref, w1b, w2b):
    b = pl.program_id(0)
    active = b < na_ref[0]
    new_expert = (b == 0) | (be_ref[b] != be_ref[jnp.maximum(b - 1, 0)])

    @pl.when(active & new_expert)
    def _():
        w1b[...] = w1_ref[...].astype(BF16)
        w2b[...] = w2_ref[...].astype(BF16)

    @pl.when(active)
    def _():
        xb = x_ref[...].astype(BF16)
        gu = jnp.dot(xb, w1b[...], preferred_element_type=F32) + b1_ref[...]
        gate = jnp.minimum(gu, SWIGLU_LIMIT)
        a = gate * jax.nn.sigmoid(SWIGLU_ALPHA * gate)
        u = jnp.clip(gu, -SWIGLU_LIMIT, SWIGLU_LIMIT) + 1.0
        act = a * pltpu.roll(u, u.shape[1] - 1, 1)
        act = jnp.dot(act.astype(BF16), sel_ref[...], preferred_element_type=F32)
        y_ref[...] = jnp.dot(act.astype(BF16), w2b[...], preferred_element_type=F32) + b2_ref[...]


def _experts(block_expert, n_active, xs, w1, b1, w2, b2, sel):
    n_slots, d = xs.shape
    bm = MOE_ROWS
    nb = n_slots // bm
    blk = lambda b, be, na: (jnp.minimum(b, na[0] - 1), 0)
    wsel = lambda b, be, na: (be[jnp.minimum(b, na[0] - 1)], 0, 0)
    grid_spec = pltpu.PrefetchScalarGridSpec(
        num_scalar_prefetch=2,
        grid=(nb,),
        in_specs=[pl.BlockSpec((bm, d), blk),
                  pl.BlockSpec((None, d, 2 * D_EXPERT), wsel),
                  pl.BlockSpec((None, 1, 2 * D_EXPERT), wsel),
                  pl.BlockSpec((None, D_EXPERT, d), wsel),
                  pl.BlockSpec((None, 1, d), wsel),
                  pl.BlockSpec((2 * D_EXPERT, D_EXPERT), lambda b, be, na: (0, 0))],
        out_specs=pl.BlockSpec((bm, d), blk),
        scratch_shapes=[pltpu.VMEM((d, 2 * D_EXPERT), BF16), pltpu.VMEM((D_EXPERT, d), BF16)],
    )
    return pl.pallas_call(
        _expert_kernel,
        grid_spec=grid_spec,
        out_shape=jax.ShapeDtypeStruct((n_slots, d), F32),
        compiler_params=_cparams(("arbitrary",)),
        name="moe_experts",
    )(block_expert, n_active, xs, w1, b1, w2, b2, sel)


def _combine_kernel(pstart_ref, te_ref, rk_ref, x_ref, gt_ref, mod_ref, y_ref, o_ref, buf, sem):
    tm = x_ref.shape[0]

    def row_copy(r, k):
        a = r * TOP_K + k
        src = pstart_ref[te_ref[a]] + rk_ref[a]
        return pltpu.make_async_copy(y_ref.at[pl.ds(src, 1), :], buf.at[k, pl.ds(r, 1), :], sem)

    def issue(r, c):
        for k in range(TOP_K):
            row_copy(r, k).start()
        return c

    def drain(r, c):
        for k in range(TOP_K):
            row_copy(r, k).wait()
        return c

    lax.fori_loop(0, tm, issue, 0)
    lax.fori_loop(0, tm, drain, 0)
    gt = gt_ref[...]
    f = gt[:, 0:1] * buf[0]
    for k in range(1, TOP_K):
        f = f + gt[:, k:k + 1] * buf[k]
    o_ref[...] = x_ref[...] + mod_ref[5:6, :] * f


def _combine(pstart, te_flat, rk_flat, xs, gates, mod, y, n_lat_tiles):
    t, d = xs.shape
    tm = ROW_TILE
    flat = pl.BlockSpec((tm * TOP_K,), lambda i, ps: (i,), memory_space=pltpu.SMEM)
    grid_spec = pltpu.PrefetchScalarGridSpec(
        num_scalar_prefetch=1,
        grid=(t // tm,),
        in_specs=[flat, flat,
                  pl.BlockSpec((tm, d), lambda i, ps: (i, 0)),
                  pl.BlockSpec((tm, LANES), lambda i, ps: (i, 0)),
                  pl.BlockSpec((None, N_MOD, d), lambda i, ps: (jnp.where(i < n_lat_tiles, 0, 1), 0, 0)),
                  pl.BlockSpec(memory_space=pl.ANY)],
        out_specs=pl.BlockSpec((tm, d), lambda i, ps: (i, 0)),
        scratch_shapes=[pltpu.VMEM((TOP_K, tm, d), F32), pltpu.SemaphoreType.DMA(())],
    )
    return pl.pallas_call(
        _combine_kernel,
        grid_spec=grid_spec,
        out_shape=jax.ShapeDtypeStruct((t, d), F32),
        compiler_params=_cparams(("arbitrary",)),
        name="moe_combine",
    )(pstart, te_flat, rk_flat, xs, gates, mod, y)


def _final_norm_kernel(x_ref, g_ref, o_ref):
    o_ref[...] = _rms(x_ref[...]) * g_ref[...]


def _final_norm(xs, g, n_lat):
    d = xs.shape[1]
    tm = ROW_TILE
    return pl.pallas_call(
        _final_norm_kernel,
        grid=(n_lat // tm,),
        in_specs=[pl.BlockSpec((tm, d), lambda i: (i, 0)), pl.BlockSpec((1, d), lambda i: (0, 0))],
        out_specs=pl.BlockSpec((tm, d), lambda i: (i, 0)),
        out_shape=jax.ShapeDtypeStruct((n_lat, d), F32),
        compiler_params=_cparams(("arbitrary",)),
        name="final_norm",
    )(xs, g)


def _rope_tables(n_lat, n_ctx):
    rows = n_lat // GRID_W
    row_id, col_id = jnp.meshgrid(jnp.arange(rows), jnp.arange(GRID_W), indexing="ij")
    half = MLA_ROPE // 2
    inv_freq = ROPE_THETA ** (-jnp.arange(0, half, 2, dtype=F32) / half)

    def axis_angles(pos):
        a = pos.reshape(-1).astype(F32)[:, None] * inv_freq[None, :]
        return jnp.concatenate([a, a], axis=-1)

    ang = jnp.concatenate([axis_angles(row_id), axis_angles(col_id)], axis=-1)
    cos, sin = jnp.cos(ang), jnp.sin(ang)
    sign = jnp.where((jnp.arange(MLA_ROPE) % 32) < 16, -1.0, 1.0).astype(F32)
    sin = sin * sign[None, :]
    cos = jnp.concatenate([cos, jnp.ones((n_ctx, MLA_ROPE), F32)], axis=0)
    sin = jnp.concatenate([sin, jnp.zeros((n_ctx, MLA_ROPE), F32)], axis=0)
    return jnp.concatenate([cos, cos], axis=1), jnp.concatenate([sin, sin], axis=1)


def _split_in_proj(w):
    d = w.shape[0]
    sizes = (MLA_Q_RANK, MLA_KV_RANK, MLA_ROPE, SWA_WIDTH, SWA_KV_WIDTH, SWA_KV_WIDTH,
             CONV_WIDTH, CONV_WIDTH, CONV_WIDTH)
    offs = np.concatenate([[0], np.cumsum(sizes)])
    part = [w[:, offs[j]:offs[j + 1]] for j in range(len(sizes))]
    cq, ckv, kr, sq, sk, sv, cb, cc, cx = part
    return jnp.concatenate([cq, ckv, sq, sk, sv, cb, cc, cx, kr, jnp.zeros((d, LANES - MLA_ROPE), w.dtype)],
                           axis=1).astype(BF16)


def _split_uq(w):
    r = w.shape[0]
    w = w.reshape(r, MLA_HEADS, MLA_NOPE + MLA_ROPE)
    wn = w[:, :, :MLA_NOPE].reshape(r, MLA_HEADS * MLA_NOPE)
    wr = jnp.concatenate([w[:, :, MLA_NOPE:], jnp.zeros((r, MLA_HEADS, LANES - MLA_ROPE), w.dtype)], axis=2)
    return wn.astype(BF16), wr.reshape(r, MLA_HEADS * LANES).astype(BF16)


def _split_hi_lo(w):
    hi = w.astype(BF16)
    lo = (w - hi.astype(F32)).astype(BF16)
    return jnp.concatenate([hi, lo], axis=1)


def kernel(x, c, ctx, c_ctx, w_ada, b_ada, g_mix, w_in, g_mla_q, g_mla_kv, w_mla_uq, w_mla_ukv,
           swa_sink, conv_w, g_out, w_out, g_ffn, w_router, b_router, w_exp1, b_exp1, w_exp2,
           b_exp2, g_final):
    bsz, n_lat, d = x.shape
    n_ctx = ctx.shape[1]
    depth = w_ada.shape[0]
    assert bsz == 1 and n_lat % ATTN_K_CHUNK == 0 and n_ctx == ROW_TILE and d % LANES == 0
    t = n_lat + n_ctx
    t_pad = -(-t // ATTN_K_CHUNK) * ATTN_K_CHUNK
    n_lat_tiles = n_lat // ROW_TILE

    xs = jnp.concatenate([x[0], ctx[0]], axis=0)
    cos, sin = _rope_tables(n_lat, n_ctx)
    mods = _ada(c, c_ctx, w_ada, b_ada).reshape(depth, 2, N_MOD, d)

    n_blocks = -(-t * TOP_K // MOE_ROWS) + N_EXPERTS
    n_slots = n_blocks * MOE_ROWS
    even_sel = (jnp.arange(2 * D_EXPERT)[:, None] == 2 * jnp.arange(D_EXPERT)[None, :]).astype(BF16)

    for l in range(depth):
        mod = mods[l]
        cq, ckv, kr, sq, sk, sv, cb, cv = _inproj(xs, g_mix[l][None], mod, cos, sin,
                                                  _split_in_proj(w_in[l]), n_lat_tiles)
        wqn, wqr = _split_uq(w_mla_uq[l])
        q, k, v = _mla_up(cq, ckv, kr, g_mla_q[l][None], g_mla_kv[l][None], cos, sin,
                          wqn, wqr, w_mla_ukv[l].astype(BF16), t_pad)
        mla = _mla_attention(q, k, v, n_lat, n_ctx)
        swa = _swa(swa_sink[l], sq, sk, sv, n_lat, n_ctx)
        xs, h2, top_e, gates, rank, counts = _mix_out(
            xs, mla, swa, cb, cv, conv_w[l], g_out[l][None], w_out[l].astype(BF16), mod,
            g_ffn[l][None], _split_hi_lo(w_router[l]), b_router[l][None], n_lat_tiles)

        cnt = counts[0]
        padded = (cnt + MOE_ROWS - 1) // MOE_ROWS * MOE_ROWS
        pend = jnp.cumsum(padded).astype(jnp.int32)
        pstart = (pend - padded).astype(jnp.int32)
        bstart = jnp.arange(n_blocks, dtype=jnp.int32) * MOE_ROWS
        block_expert = jnp.minimum(jnp.sum((pend[None, :] <= bstart[:, None]).astype(jnp.int32), axis=1),
                                   N_EXPERTS - 1).astype(jnp.int32)
        n_active = (pend[-1:] // MOE_ROWS).astype(jnp.int32)
        te_flat = top_e[:, :TOP_K].reshape(-1)
        rk_flat = rank[:, :TOP_K].reshape(-1)

        xg = _dispatch(pstart, pend, te_flat, rk_flat, h2, n_slots)
        y = _experts(block_expert, n_active, xg, w_exp1[l], b_exp1[l][:, None, :],
                     w_exp2[l], b_exp2[l][:, None, :], even_sel)
        xs = _combine(pstart, te_flat, rk_flat, xs, gates, mod, y, n_lat_tiles)

    return _final_norm(xs, g_final[None], n_lat).reshape(1, n_lat, d)
```

```python
import functools

import numpy as np
import jax
import jax.numpy as jnp
from jax import lax
from jax.experimental import pallas as pl
from jax.experimental.pallas import tpu as pltpu

F32 = jnp.float32
BF16 = jnp.bfloat16

GRID_W = 64
ROPE_THETA = 10000.0
NORM_EPS = 1e-6
N_MOD = 6
MLA_HEADS = 8
MLA_Q_RANK = 512
MLA_KV_RANK = 256
MLA_NOPE = 128
MLA_ROPE = 64
MLA_V = 128
MLA_WIDTH = MLA_HEADS * MLA_V
MLA_SCALE = (MLA_NOPE + MLA_ROPE) ** -0.5
SWA_HEADS = 8
SWA_KV_HEADS = 2
SWA_GROUP = SWA_HEADS // SWA_KV_HEADS
SWA_HEAD_DIM = 64
SWA_WINDOW = 128
SWA_WIDTH = SWA_HEADS * SWA_HEAD_DIM
SWA_KV_WIDTH = SWA_KV_HEADS * SWA_HEAD_DIM
SWA_SCALE = SWA_HEAD_DIM ** -0.5
CONV_WIDTH = 512
CONV_K = 3
MIX_WIDTH = MLA_WIDTH + SWA_WIDTH + CONV_WIDTH
N_EXPERTS = 32
TOP_K = 4
D_EXPERT = 512
SWIGLU_LIMIT = 7.0
SWIGLU_ALPHA = 1.702
LOG2E = 1.4426950408889634

LANES = 128
SUBLANES = 8
VMEM_LIMIT = 56 * 1024 * 1024

ROW_TILE = 256
SWA_TILE = 128
MOE_ROWS = 256
ATTN_Q_TILE = 1024
ATTN_K_CHUNK = 512
ATTN_UNROLL = 8
PAD_KEY_SCORE = -1e30

_O_CQ = 0
_O_CKV = _O_CQ + MLA_Q_RANK
_O_SQ = _O_CKV + MLA_KV_RANK
_O_SK = _O_SQ + SWA_WIDTH
_O_SV = _O_SK + SWA_KV_WIDTH
_O_CB = _O_SV + SWA_KV_WIDTH
_O_CC = _O_CB + CONV_WIDTH
_O_CX = _O_CC + CONV_WIDTH
_O_KR = _O_CX + CONV_WIDTH
N_IN_PAD = _O_KR + LANES


def _cparams(sem):
    return pltpu.CompilerParams(dimension_semantics=sem, vmem_limit_bytes=VMEM_LIMIT)


def _rms(x):
    return x * lax.rsqrt(jnp.mean(x * x, axis=-1, keepdims=True) + NORM_EPS)


def _slab_pieces(d):
    return d // LANES


def _slab_rows(d):
    n = _slab_pieces(d)
    return n + 4 if n % SUBLANES == 0 else n


def _store_slabs(ref, val):
    rows, d = val.shape
    sr = _slab_rows(d)
    for j in range(_slab_pieces(d)):
        ref[pl.ds(j, rows, stride=sr), :] = val[:, j * LANES:(j + 1) * LANES]


def _load_slab_piece(ref, j, rows, sr):
    return ref[pl.ds(j, rows, stride=sr), :]


def _rope(u, cos, sin_signed):
    w = u.shape[-1]
    reps = w // LANES
    if reps > 1:
        cos = jnp.concatenate([cos] * reps, axis=1)
        sin_signed = jnp.concatenate([sin_signed] * reps, axis=1)
    lane = lax.broadcasted_iota(jnp.int32, u.shape, 1)
    first = (lane % 32) < 16
    rot = jnp.where(first, pltpu.roll(u, w - 16, 1), pltpu.roll(u, 16, 1))
    return u * cos + rot * sin_signed


def _ada_kernel(s_ref, w_ref, b_ref, o_ref, *, chunk):
    d, tn = w_ref.shape

    def body(i, accs):
        a0, a1 = accs
        r0 = pl.multiple_of(i * chunk, chunk)
        w = w_ref[pl.ds(r0, chunk), :]
        s = s_ref[pl.ds(r0, chunk), :]
        s = s * jax.nn.sigmoid(s)
        p0 = (w * s[:, 0:1]).reshape(chunk // SUBLANES, SUBLANES, tn).sum(axis=0)
        p1 = (w * s[:, 1:2]).reshape(chunk // SUBLANES, SUBLANES, tn).sum(axis=0)
        return a0 + p0, a1 + p1

    z = jnp.zeros((SUBLANES, tn), F32)
    a0, a1 = lax.fori_loop(0, d // chunk, body, (z, z))
    b = b_ref[...]
    o_ref[0:1, :] = jnp.sum(a0, axis=0, keepdims=True) + b
    o_ref[1:2, :] = jnp.sum(a1, axis=0, keepdims=True) + b


def _ada(c, c_ctx, w_ada, b_ada):
    depth, d, n = w_ada.shape
    tn = 1024 if n % 1024 == 0 else 512
    chunk = 64
    s = jnp.stack([c.reshape(d), c_ctx.reshape(d)], axis=1)
    return pl.pallas_call(
        functools.partial(_ada_kernel, chunk=chunk),
        grid=(depth, n // tn),
        in_specs=[
            pl.BlockSpec((d, 2), lambda l, j: (0, 0)),
            pl.BlockSpec((None, d, tn), lambda l, j: (l, 0, j)),
            pl.BlockSpec((None, 1, tn), lambda l, j: (l, 0, j)),
        ],
        out_specs=pl.BlockSpec((None, 2, tn), lambda l, j: (l, 0, j)),
        out_shape=jax.ShapeDtypeStruct((depth, 2, n), F32),
        compiler_params=_cparams(("arbitrary", "arbitrary")),
        name="ada_mod",
    )(s, w_ada, b_ada.reshape(depth, 1, n))


def _inproj_kernel(x_ref, g_ref, mod_ref, cos_ref, sin_ref, w_ref,
                   cq_ref, ckv_ref, kr_ref, sq_ref, sk_ref, sv_ref, cb_ref, cv_ref):
    x = x_ref[...]
    h = _rms(x) * g_ref[...]
    h = h * (1.0 + mod_ref[1:2, :]) + mod_ref[0:1, :]
    hb = h.astype(BF16)
    cos = cos_ref[...]
    sin = sin_ref[...]

    def proj(a, width):
        return jnp.dot(hb, w_ref[:, a:a + width], preferred_element_type=F32)

    cq_ref[...] = proj(_O_CQ, MLA_Q_RANK)
    ckv_ref[...] = proj(_O_CKV, MLA_KV_RANK)
    kr_ref[...] = _rope(proj(_O_KR, LANES), cos, sin).astype(BF16)
    sq_ref[...] = (_rope(proj(_O_SQ, SWA_WIDTH), cos, sin) * SWA_SCALE).astype(BF16)
    sk_ref[...] = _rope(proj(_O_SK, SWA_KV_WIDTH), cos, sin).astype(BF16)
    sv_ref[...] = proj(_O_SV, SWA_KV_WIDTH).astype(BF16)
    cb_ref[...] = proj(_O_CB, CONV_WIDTH)
    cv_ref[...] = proj(_O_CC, CONV_WIDTH) * proj(_O_CX, CONV_WIDTH)


def _inproj(xs, g, mod, cos, sin, w_in_p, n_lat_tiles):
    t, d = xs.shape
    tm = ROW_TILE
    row = lambda w: pl.BlockSpec((tm, w), lambda i: (i, 0))
    outs = [
        (MLA_Q_RANK, F32), (MLA_KV_RANK, F32), (LANES, BF16), (SWA_WIDTH, BF16),
        (SWA_KV_WIDTH, BF16), (SWA_KV_WIDTH, BF16), (CONV_WIDTH, F32), (CONV_WIDTH, F32),
    ]
    return pl.pallas_call(
        _inproj_kernel,
        grid=(t // tm,),
        in_specs=[
            row(d),
            pl.BlockSpec((1, d), lambda i: (0, 0)),
            pl.BlockSpec((None, N_MOD, d), lambda i: (jnp.where(i < n_lat_tiles, 0, 1), 0, 0)),
            row(LANES), row(LANES),
            pl.BlockSpec((d, N_IN_PAD), lambda i: (0, 0)),
        ],
        out_specs=[row(w) for w, _ in outs],
        out_shape=[jax.ShapeDtypeStruct((t, w), dt) for w, dt in outs],
        compiler_params=_cparams(("arbitrary",)),
        name="mixer_in_proj",
    )(xs, g, mod, cos, sin, w_in_p)


def _mla_up_kernel(cq_ref, ckv_ref, kr_ref, gq_ref, gkv_ref, cos_ref, sin_ref,
                   wqn_ref, wqr_ref, wkv_ref, q_ref, k_ref, v_ref, *, n_real_tiles):
    i = pl.program_id(0)
    tm = cq_ref.shape[0]
    last_lane = lax.broadcasted_iota(jnp.int32, (tm, LANES), 1) == LANES - 1

    @pl.when(i < n_real_tiles)
    def _():
        hq = (_rms(cq_ref[...]) * gq_ref[...]).astype(BF16)
        qs = MLA_SCALE * LOG2E
        qn = jnp.dot(hq, wqn_ref[...], preferred_element_type=F32) * qs
        qr = _rope(jnp.dot(hq, wqr_ref[...], preferred_element_type=F32), cos_ref[...], sin_ref[...]) * qs
        hk = (_rms(ckv_ref[...]) * gkv_ref[...]).astype(BF16)
        kv = jnp.dot(hk, wkv_ref[...], preferred_element_type=F32)
        kr = kr_ref[...]
        ones = jnp.ones((tm, LANES), BF16)
        for h in range(MLA_HEADS):
            q_ref[h, :, 0:LANES] = qn[:, h * LANES:(h + 1) * LANES].astype(BF16)
            q_ref[h, :, LANES:2 * LANES] = jnp.where(last_lane, 1.0, qr[:, h * LANES:(h + 1) * LANES]).astype(BF16)
            k_ref[h, :, 0:LANES] = kv[:, h * 2 * LANES:h * 2 * LANES + LANES].astype(BF16)
            k_ref[h, :, LANES:2 * LANES] = kr
            v_ref[h, :, 0:LANES] = kv[:, h * 2 * LANES + LANES:(h + 1) * 2 * LANES].astype(BF16)
            v_ref[h, :, LANES:2 * LANES] = ones

    @pl.when(i >= n_real_tiles)
    def _():
        q_ref[...] = jnp.zeros(q_ref.shape, BF16)
        v_ref[...] = jnp.zeros(v_ref.shape, BF16)
        zero = jnp.zeros((tm, LANES), BF16)
        flag = jnp.where(last_lane, PAD_KEY_SCORE, 0.0).astype(BF16)
        for h in range(MLA_HEADS):
            k_ref[h, :, 0:LANES] = zero
            k_ref[h, :, LANES:2 * LANES] = flag


def _mla_up(cq, ckv, kr, gq, gkv, cos, sin, wqn, wqr, wkv, t_pad):
    t = cq.shape[0]
    tm = ROW_TILE
    nr = t // tm
    row = lambda w: pl.BlockSpec((tm, w), lambda i: (jnp.minimum(i, nr - 1), 0))
    full = lambda a: pl.BlockSpec(a.shape, lambda i: (0,) * a.ndim)
    hd = pl.BlockSpec((MLA_HEADS, tm, 2 * LANES), lambda i: (0, i, 0))
    shp = jax.ShapeDtypeStruct((MLA_HEADS, t_pad, 2 * LANES), BF16)
    return pl.pallas_call(
        functools.partial(_mla_up_kernel, n_real_tiles=nr),
        grid=(t_pad // tm,),
        in_specs=[row(MLA_Q_RANK), row(MLA_KV_RANK), row(LANES), full(gq), full(gkv),
                  row(LANES), row(LANES), full(wqn), full(wqr), full(wkv)],
        out_specs=[hd, hd, hd],
        out_shape=[shp, shp, shp],
        compiler_params=_cparams(("arbitrary",)),
        name="mla_up_proj",
    )(cq, ckv, kr, gq, gkv, cos, sin, wqn, wqr, wkv)


def _mla_attn_kernel(q_ref, k_ref, v_ref, o_ref, s_buf, m_sc, acc_sc, *, n_chunks, ch, unroll):
    reps = ch // LANES
    m_sc[...] = jnp.full(m_sc.shape, -jnp.inf, F32)
    acc_sc[...] = jnp.zeros(acc_sc.shape, F32)

    def scores(c, slot):
        kc = k_ref[pl.ds(pl.multiple_of(c * ch, ch), ch), :]
        s_buf[slot] = lax.dot_general(q_ref[...], kc, (((1,), (1,)), ((), ())), preferred_element_type=F32)

    def softmax_pv(c, slot):
        s = s_buf[slot]
        m_prev = m_sc[...]
        m_new = jnp.maximum(m_prev, jnp.max(s, axis=-1, keepdims=True))
        alpha = jnp.exp2(m_prev - m_new)
        p = jnp.exp2(s - jnp.concatenate([m_new] * reps, axis=1)).astype(BF16)
        vc = v_ref[pl.ds(pl.multiple_of(c * ch, ch), ch), :]
        pv = jnp.dot(p, vc, preferred_element_type=F32)
        acc_sc[...] = acc_sc[...] * jnp.concatenate([alpha, alpha], axis=1) + pv
        m_sc[...] = m_new

    scores(0, 0)

    def trip(j, carry):
        for u in range(unroll):
            scores(j * unroll + u + 1, (u + 1) % 2)
            softmax_pv(j * unroll + u, u % 2)
        return carry

    n_trips = (n_chunks - 1) // unroll
    lax.fori_loop(0, n_trips, trip, 0)
    for c in range(n_trips * unroll, n_chunks):
        if c + 1 < n_chunks:
            scores(c + 1, (c + 1) % 2)
        softmax_pv(c, c % 2)
    acc = acc_sc[...]
    o_ref[...] = acc[:, 0:MLA_V] / acc[:, MLA_V:2 * MLA_V]


def _mla_ctx_kernel(q_ref, k_ref, v_ref, prev_ref, o_ref):
    del prev_ref
    s = lax.dot_general(q_ref[...], k_ref[...], (((1,), (1,)), ((), ())), preferred_element_type=F32)
    p = jnp.exp2(s - jnp.max(s, axis=-1, keepdims=True))
    pv = jnp.dot(p.astype(BF16), v_ref[...], preferred_element_type=F32)
    o_ref[...] = pv[:, 0:MLA_V] / pv[:, MLA_V:2 * MLA_V]


def _mla_attention(q, k, v, n_lat, n_ctx):
    t = n_lat + n_ctx
    t_pad = k.shape[1]
    tq = ATTN_Q_TILE if n_lat % ATTN_Q_TILE == 0 else ROW_TILE
    ch = ATTN_K_CHUNK
    wide = 2 * LANES
    out = pl.pallas_call(
        functools.partial(_mla_attn_kernel, n_chunks=t_pad // ch, ch=ch, unroll=ATTN_UNROLL),
        grid=(MLA_HEADS, n_lat // tq),
        in_specs=[pl.BlockSpec((None, tq, wide), lambda h, i: (h, i, 0)),
                  pl.BlockSpec((None, t_pad, wide), lambda h, i: (h, 0, 0)),
                  pl.BlockSpec((None, t_pad, wide), lambda h, i: (h, 0, 0))],
        out_specs=pl.BlockSpec((tq, MLA_V), lambda h, i: (i, h)),
        out_shape=jax.ShapeDtypeStruct((t, MLA_WIDTH), F32),
        scratch_shapes=[pltpu.VMEM((2, tq, ch), F32), pltpu.VMEM((tq, LANES), F32), pltpu.VMEM((tq, wide), F32)],
        compiler_params=_cparams(("arbitrary", "arbitrary")),
        name="mla_attention",
    )(q, k, v)
    cb = n_lat // n_ctx
    return pl.pallas_call(
        _mla_ctx_kernel,
        grid=(MLA_HEADS,),
        in_specs=[pl.BlockSpec((None, n_ctx, wide), lambda h: (h, cb, 0)),
                  pl.BlockSpec((None, n_ctx, wide), lambda h: (h, cb, 0)),
                  pl.BlockSpec((None, n_ctx, wide), lambda h: (h, cb, 0)),
                  pl.BlockSpec(memory_space=pl.ANY)],
        out_specs=pl.BlockSpec((n_ctx, MLA_V), lambda h: (cb, h)),
        out_shape=jax.ShapeDtypeStruct((t, MLA_WIDTH), F32),
        input_output_aliases={3: 0},
        compiler_params=_cparams(("arbitrary",)),
        name="mla_attention_ctx",
    )(q, k, v, out)


def _swa_kernel(sink_ref, q_ref, kp_ref, ko_ref, kn_ref, kc_ref, vp_ref, vo_ref, vn_ref, vc_ref, o_ref,
                *, n_lat_tiles, n_lat, n_ctx):
    i = pl.program_id(0)
    tb = SWA_TILE
    nk = n_ctx + 3 * tb
    kall = jnp.concatenate([kc_ref[...], kp_ref[...], ko_ref[...], kn_ref[...]], axis=0)
    vall = jnp.concatenate([vc_ref[...], vp_ref[...], vo_ref[...], vn_ref[...]], axis=0)
    col = lax.broadcasted_iota(jnp.int32, (tb, nk), 1)
    r = lax.broadcasted_iota(jnp.int32, (tb, nk), 0)
    j = col - n_ctx
    kpos = (i - 1) * tb + j
    local_ok = (jnp.abs(j - tb - r) <= SWA_WINDOW) & (kpos >= 0) & (kpos < n_lat) & (i < n_lat_tiles)
    valid = (col < n_ctx) | local_ok
    lane_kv = lax.broadcasted_iota(jnp.int32, (nk, LANES), 1)
    k_roll = pltpu.roll(kall, SWA_HEAD_DIM, 1)
    v_roll = pltpu.roll(vall, SWA_HEAD_DIM, 1)
    gw = SWA_GROUP * SWA_HEAD_DIM
    lane_g = lax.broadcasted_iota(jnp.int32, (tb, gw), 1) // SWA_HEAD_DIM
    lane_vg = lax.broadcasted_iota(jnp.int32, (nk, gw), 1) // SWA_HEAD_DIM
    for kvh in range(SWA_KV_HEADS):
        lo = lane_kv < SWA_HEAD_DIM
        if kvh == 0:
            k2 = jnp.where(lo, kall, k_roll)
            v2 = jnp.where(lo, vall, v_roll)
        else:
            k2 = jnp.where(lo, k_roll, kall)
            v2 = jnp.where(lo, v_roll, vall)
        kt = jnp.concatenate([k2, k2], axis=1)
        vt = jnp.concatenate([v2, v2], axis=1)
        qg = q_ref[:, kvh * gw:(kvh + 1) * gw]
        acc = jnp.zeros((tb, gw), F32)
        for g in range(SWA_GROUP):
            qm = jnp.where(lane_g == g, qg, jnp.zeros_like(qg))
            s = lax.dot_general(qm, kt, (((1,), (1,)), ((), ())), preferred_element_type=F32)
            s = jnp.where(valid, s, -jnp.inf)
            sink = sink_ref[kvh * SWA_GROUP + g]
            m = jnp.maximum(jnp.max(s, axis=-1, keepdims=True), sink)
            p = jnp.exp(s - m)
            denom = jnp.sum(p, axis=-1, keepdims=True) + jnp.exp(sink - m)
            p = (p / denom).astype(BF16)
            vm = jnp.where(lane_vg == g, vt, jnp.zeros_like(vt))
            acc = acc + jnp.dot(p, vm, preferred_element_type=F32)
        o_ref[:, kvh * gw:(kvh + 1) * gw] = acc


def _swa(sink, sq, sk, sv, n_lat, n_ctx):
    t = n_lat + n_ctx
    tb = SWA_TILE
    nt = t // tb
    nlt = n_lat // tb
    kvw = SWA_KV_WIDTH
    prev_spec = pl.BlockSpec((tb, kvw), lambda i: (jnp.maximum(i - 1, 0), 0))
    own_spec = pl.BlockSpec((tb, kvw), lambda i: (i, 0))
    next_spec = pl.BlockSpec((tb, kvw), lambda i: (jnp.minimum(i + 1, nt - 1), 0))
    ctx_spec = pl.BlockSpec((n_ctx, kvw), lambda i: (n_lat // n_ctx, 0))
    return pl.pallas_call(
        functools.partial(_swa_kernel, n_lat_tiles=nlt, n_lat=n_lat, n_ctx=n_ctx),
        grid=(nt,),
        in_specs=[pl.BlockSpec(memory_space=pltpu.SMEM),
                  pl.BlockSpec((tb, SWA_WIDTH), lambda i: (i, 0)),
                  prev_spec, own_spec, next_spec, ctx_spec,
                  prev_spec, own_spec, next_spec, ctx_spec],
        out_specs=pl.BlockSpec((tb, SWA_WIDTH), lambda i: (i, 0)),
        out_shape=jax.ShapeDtypeStruct((t, SWA_WIDTH), F32),
        compiler_params=_cparams(("arbitrary",)),
        name="swa_attention",
    )(sink, sq, sk, sk, sk, sk, sv, sv, sv, sv)


def _mix_out_kernel(x_ref, mla_ref, swa_ref, cb_ref, cv_ref, hp_ref, hn_ref, cw_ref, go_ref, wo_ref,
                    mod_ref, gf_ref, wr_ref, br_ref,
                    xo_ref, h2_ref, te_ref, gt_ref, rk_ref, cnt_ref, cnt_sc,
                    *, n_lat_tiles, n_tiles):
    i = pl.program_id(0)
    tm = x_ref.shape[0]

    @pl.when(i == 0)
    def _():
        cnt_sc[...] = jnp.zeros(cnt_sc.shape, F32)

    seg_start = (i == 0) | (i == n_lat_tiles)
    seg_end = (i == n_lat_tiles - 1) | (i == n_tiles - 1)
    v = cv_ref[...]
    rowi = lax.broadcasted_iota(jnp.int32, v.shape, 0)
    left = jnp.where(seg_start, 0.0, hp_ref[SUBLANES - 1:SUBLANES, :])
    right = jnp.where(seg_end, 0.0, hn_ref[0:1, :])
    v_dn = jnp.where(rowi == 0, left, pltpu.roll(v, 1, 0))
    v_up = jnp.where(rowi == tm - 1, right, pltpu.roll(v, tm - 1, 0))
    conv = cb_ref[...] * (v_dn * cw_ref[0:1, :] + v * cw_ref[1:2, :] + v_up * cw_ref[2:3, :])

    o1, o2 = MLA_WIDTH, MLA_WIDTH + SWA_WIDTH
    ya = (_rms(mla_ref[...]) * go_ref[:, 0:o1]).astype(BF16)
    yb = (_rms(swa_ref[...]) * go_ref[:, o1:o2]).astype(BF16)
    yc = (_rms(conv) * go_ref[:, o2:MIX_WIDTH]).astype(BF16)
    o = (jnp.dot(ya, wo_ref[0:o1, :], preferred_element_type=F32)
         + jnp.dot(yb, wo_ref[o1:o2, :], preferred_element_type=F32)
         + jnp.dot(yc, wo_ref[o2:MIX_WIDTH, :], preferred_element_type=F32))
    x = x_ref[...] + mod_ref[2:3, :] * o
    xo_ref[...] = x

    h2 = _rms(x) * gf_ref[...]
    h2 = h2 * (1.0 + mod_ref[4:5, :]) + mod_ref[3:4, :]
    _store_slabs(h2_ref, h2)
    h_hi = h2.astype(BF16)
    h_lo = (h2 - h_hi.astype(F32)).astype(BF16)
    parts = (jnp.dot(h_hi, wr_ref[...], preferred_element_type=F32)
             + jnp.dot(h_lo, wr_ref[...], preferred_element_type=F32))
    logits = parts[:, 0:N_EXPERTS] + parts[:, N_EXPERTS:2 * N_EXPERTS] + br_ref[...]
    e_iota = lax.broadcasted_iota(jnp.int32, logits.shape, 1)
    lane = lax.broadcasted_iota(jnp.int32, (tm, LANES), 1)
    work = logits
    tops, idxs = [], []
    for _ in range(TOP_K):
        m = jnp.max(work, axis=-1, keepdims=True)
        idx = jnp.min(jnp.where(work == m, e_iota, N_EXPERTS), axis=-1, keepdims=True)
        tops.append(m)
        idxs.append(idx)
        work = jnp.where(e_iota == idx, -jnp.inf, work)
    exps = [jnp.exp(tk_ - tops[0]) for tk_ in tops]
    denom = exps[0] + exps[1] + exps[2] + exps[3]

    onehots = [(e_iota == idx).astype(F32) for idx in idxs]
    sel = onehots[0] + onehots[1] + onehots[2] + onehots[3]
    rr = lax.broadcasted_iota(jnp.int32, (tm, tm), 0)
    cc = lax.broadcasted_iota(jnp.int32, (tm, tm), 1)
    tri = (rr > cc).astype(BF16)
    before = jnp.dot(tri, sel.astype(BF16), preferred_element_type=F32) + cnt_sc[...]
    cnt_new = cnt_sc[...] + jnp.sum(sel, axis=0, keepdims=True)
    cnt_sc[...] = cnt_new
    cnt_ref[...] = jnp.broadcast_to(cnt_new, cnt_ref.shape).astype(jnp.int32)

    te = jnp.zeros((tm, LANES), jnp.int32)
    rk = jnp.zeros((tm, LANES), jnp.int32)
    for k in range(TOP_K):
        rank_k = jnp.sum(before * onehots[k], axis=-1, keepdims=True).astype(jnp.int32)
        te = jnp.where(lane == k, idxs[k], te)
        rk = jnp.where(lane == k, rank_k, rk)
        gt_ref[:, k * LANES:(k + 1) * LANES] = jnp.broadcast_to(exps[k] / denom, (tm, LANES))
    te_ref[...] = te
    rk_ref[...] = rk


def _mix_out(xs, mla, swa, cb, cv, conv_w, g_out, w_out_b, mod, g_ffn, w_r, b_r, n_lat_tiles):
    t, d = xs.shape
    tm = ROW_TILE
    nt = t // tm
    hb = tm // SUBLANES
    row = lambda w: pl.BlockSpec((tm, w), lambda i: (i, 0))
    full = lambda a: pl.BlockSpec(a.shape, lambda i: (0,) * a.ndim)
    return pl.pallas_call(
        functools.partial(_mix_out_kernel, n_lat_tiles=n_lat_tiles, n_tiles=nt),
        grid=(nt,),
        in_specs=[row(d), row(MLA_WIDTH), row(SWA_WIDTH), row(CONV_WIDTH), row(CONV_WIDTH),
                  pl.BlockSpec((SUBLANES, CONV_WIDTH), lambda i: (jnp.maximum(i * hb - 1, 0), 0)),
                  pl.BlockSpec((SUBLANES, CONV_WIDTH), lambda i: (jnp.minimum((i + 1) * hb, nt * hb - 1), 0)),
                  full(conv_w), full(g_out), full(w_out_b),
                  pl.BlockSpec((None, N_MOD, d), lambda i: (jnp.where(i < n_lat_tiles, 0, 1), 0, 0)),
                  full(g_ffn), full(w_r), full(b_r)],
        out_specs=[row(d), pl.BlockSpec((tm * _slab_rows(d), LANES), lambda i: (i, 0)),
                   row(LANES), row(TOP_K * LANES), row(LANES),
                   pl.BlockSpec((SUBLANES, N_EXPERTS), lambda i: (0, 0))],
        out_shape=[jax.ShapeDtypeStruct((t, d), F32),
                   jax.ShapeDtypeStruct((t * _slab_rows(d), LANES), F32),
                   jax.ShapeDtypeStruct((t, LANES), jnp.int32), jax.ShapeDtypeStruct((t, TOP_K * LANES), F32),
                   jax.ShapeDtypeStruct((t, LANES), jnp.int32),
                   jax.ShapeDtypeStruct((SUBLANES, N_EXPERTS), jnp.int32)],
        scratch_shapes=[pltpu.VMEM((1, N_EXPERTS), F32)],
        compiler_params=_cparams(("arbitrary",)),
        name="mix_out_ffn_route",
    )(xs, mla, swa, cb, cv, cv, cv, conv_w, g_out, w_out_b, mod, g_ffn, w_r, b_r)


def _dispatch_kernel(pstart_ref, pend_ref, te_ref, rk_ref, h_ref, xs_ref, zero_buf, sem, zsem, *, sr, nr):
    tm = h_ref.shape[0] // sr
    bm = zero_buf.shape[0] // sr

    @pl.when(pl.program_id(0) == 0)
    def _():
        zero_buf[...] = jnp.zeros(zero_buf.shape, F32)

        def tail_copy(e):
            last = pl.multiple_of(jnp.maximum(pend_ref[e] - bm, 0) * sr, bm * sr)
            return pltpu.make_async_copy(zero_buf, xs_ref.at[pl.ds(last, bm * sr), :], zsem)

        def zstart(e, c):
            @pl.when(pend_ref[e] > pstart_ref[e])
            def _():
                tail_copy(e).start()
            return c

        def zwait(e, c):
            @pl.when(pend_ref[e] > pstart_ref[e])
            def _():
                tail_copy(e).wait()
            return c

        lax.fori_loop(0, N_EXPERTS, zstart, 0)
        lax.fori_loop(0, N_EXPERTS, zwait, 0)

    def row_copy(r, k):
        a = r * TOP_K + k
        dest = pl.multiple_of((pstart_ref[te_ref[a]] + rk_ref[a]) * sr, sr)
        return pltpu.make_async_copy(h_ref.at[pl.ds(pl.multiple_of(r * sr, sr), nr), :],
                                     xs_ref.at[pl.ds(dest, nr), :], sem)

    def issue(r, c):
        for k in range(TOP_K):
            row_copy(r, k).start()
        return c

    def drain(r, c):
        for k in range(TOP_K):
            row_copy(r, k).wait()
        return c

    lax.fori_loop(0, tm, issue, 0)
    lax.fori_loop(0, tm, drain, 0)


def _dispatch(pstart, pend, te_flat, rk_flat, h2, n_slots, sr, nr):
    w = h2.shape[1]
    tm = ROW_TILE
    flat = pl.BlockSpec((tm * TOP_K,), lambda i, ps, pe: (i,), memory_space=pltpu.SMEM)
    grid_spec = pltpu.PrefetchScalarGridSpec(
        num_scalar_prefetch=2,
        grid=(h2.shape[0] // (tm * sr),),
        in_specs=[flat, flat, pl.BlockSpec((tm * sr, w), lambda i, ps, pe: (i, 0))],
        out_specs=pl.BlockSpec(memory_space=pl.ANY),
        scratch_shapes=[pltpu.VMEM((MOE_ROWS * sr, w), F32), pltpu.SemaphoreType.DMA(()),
                        pltpu.SemaphoreType.DMA(())],
    )
    return pl.pallas_call(
        functools.partial(_dispatch_kernel, sr=sr, nr=nr),
        grid_spec=grid_spec,
        out_shape=jax.ShapeDtypeStruct((n_slots * sr, w), F32),
        compiler_params=_cparams(("arbitrary",)),
        name="moe_dispatch",
    )(pstart, pend, te_flat, rk_flat, h2)


def _expert_kernel(be_ref, na_ref, x_ref, w1_ref, b1_ref, w2_ref, b2_ref, sel_ref, y_ref, w1b, w2b):
    b = pl.program_id(0)
    active = b < na_ref[0]
    new_expert = (b == 0) | (be_ref[b] != be_ref[jnp.maximum(b - 1, 0)])

    @pl.when(active & new_expert)
    def _():
        w1b[...] = w1_ref[...].astype(BF16)
        w2b[...] = w2_ref[...].astype(BF16)

    @pl.when(active)
    def _():
        sr = _slab_rows(w1b.shape[0])
        bm = x_ref.shape[0] // sr
        gu = b1_ref[...]
        for j in range(0, _slab_pieces(w1b.shape[0]), 2):
            xj = jnp.concatenate([_load_slab_piece(x_ref, j, bm, sr), _load_slab_piece(x_ref, j + 1, bm, sr)],
                                 axis=1).astype(BF16)
            gu = gu + jnp.dot(xj, w1b[j * LANES:(j + 2) * LANES, :], preferred_element_type=F32)
        gate = jnp.minimum(gu, SWIGLU_LIMIT)
        a = gate * jax.nn.sigmoid(SWIGLU_ALPHA * gate)
        u = jnp.clip(gu, -SWIGLU_LIMIT, SWIGLU_LIMIT) + 1.0
        act = a * pltpu.roll(u, u.shape[1] - 1, 1)
        act = jnp.dot(act.astype(BF16), sel_ref[...], preferred_element_type=F32)
        y = jnp.dot(act.astype(BF16), w2b[...], preferred_element_type=F32) + b2_ref[...]
        _store_slabs(y_ref, y)


def _experts(layer, block_expert, n_active, xs, w1, b1, w2, b2, sel):
    d = w1.shape[2]
    sr = _slab_rows(d)
    w = xs.shape[1]
    bm = MOE_ROWS
    nb = xs.shape[0] // (bm * sr)
    blk = lambda b, be, na: (jnp.minimum(b, na[0] - 1), 0)
    wsel = lambda b, be, na: (layer, be[jnp.minimum(b, na[0] - 1)], 0, 0)
    grid_spec = pltpu.PrefetchScalarGridSpec(
        num_scalar_prefetch=2,
        grid=(nb,),
        in_specs=[pl.BlockSpec((bm * sr, w), blk),
                  pl.BlockSpec((None, None, d, 2 * D_EXPERT), wsel),
                  pl.BlockSpec((None, None, 1, 2 * D_EXPERT), wsel),
                  pl.BlockSpec((None, None, D_EXPERT, d), wsel),
                  pl.BlockSpec((None, None, 1, d), wsel),
                  pl.BlockSpec((2 * D_EXPERT, D_EXPERT), lambda b, be, na: (0, 0))],
        out_specs=pl.BlockSpec((bm * sr, w), blk),
        scratch_shapes=[pltpu.VMEM((d, 2 * D_EXPERT), BF16), pltpu.VMEM((D_EXPERT, d), BF16)],
    )
    return pl.pallas_call(
        _expert_kernel,
        grid_spec=grid_spec,
        out_shape=jax.ShapeDtypeStruct(xs.shape, F32),
        compiler_params=_cparams(("arbitrary",)),
        name="moe_experts",
    )(block_expert, n_active, xs, w1, b1, w2, b2, sel)


def _combine_kernel(pstart_ref, te_ref, rk_ref, x_ref, gt_ref, mod_ref, y_ref, *rest, final):
    if final:
        gfin_ref, o_ref, buf, sem = rest
    else:
        o_ref, buf, sem = rest
    tm, d = x_ref.shape
    sr = _slab_rows(d)
    nr = _slab_pieces(d)

    def row_copy(r, k):
        a = r * TOP_K + k
        src = pl.multiple_of((pstart_ref[te_ref[a]] + rk_ref[a]) * sr, sr)
        return pltpu.make_async_copy(y_ref.at[pl.ds(src, nr), :],
                                     buf.at[k, pl.ds(pl.multiple_of(r * sr, sr), nr), :], sem)

    def issue(r, c):
        for k in range(TOP_K):
            row_copy(r, k).start()
        return c

    def drain(r, c):
        for k in range(TOP_K):
            row_copy(r, k).wait()
        return c

    lax.fori_loop(0, tm, issue, 0)
    lax.fori_loop(0, tm, drain, 0)
    for j in range(nr):
        f = gt_ref[:, 0:LANES] * _load_slab_piece(buf.at[0], j, tm, sr)
        for k in range(1, TOP_K):
            f = f + gt_ref[:, k * LANES:(k + 1) * LANES] * _load_slab_piece(buf.at[k], j, tm, sr)
        cols = slice(j * LANES, (j + 1) * LANES)
        o_ref[:, cols] = x_ref[:, cols] + mod_ref[5:6, cols] * f
    if final:
        o_ref[...] = _rms(o_ref[...]) * gfin_ref[...]


def _combine(pstart, te_flat, rk_flat, xs, gates, mod, y, n_lat_tiles, g_final=None):
    t, d = xs.shape
    tm = ROW_TILE
    n_tiles = t // tm if g_final is None else n_lat_tiles
    flat = pl.BlockSpec((tm * TOP_K,), lambda i, ps: (i,), memory_space=pltpu.SMEM)
    in_specs = [flat, flat,
                pl.BlockSpec((tm, d), lambda i, ps: (i, 0)),
                pl.BlockSpec((tm, TOP_K * LANES), lambda i, ps: (i, 0)),
                pl.BlockSpec((None, N_MOD, d), lambda i, ps: (jnp.where(i < n_lat_tiles, 0, 1), 0, 0)),
                pl.BlockSpec(memory_space=pl.ANY)]
    args = [pstart, te_flat, rk_flat, xs, gates, mod, y]
    if g_final is not None:
        in_specs.append(pl.BlockSpec((1, d), lambda i, ps: (0, 0)))
        args.append(g_final)
    grid_spec = pltpu.PrefetchScalarGridSpec(
        num_scalar_prefetch=1,
        grid=(n_tiles,),
        in_specs=in_specs,
        out_specs=pl.BlockSpec((tm, d), lambda i, ps: (i, 0)),
        scratch_shapes=[pltpu.VMEM((TOP_K, tm * _slab_rows(d), LANES), F32), pltpu.SemaphoreType.DMA(())],
    )
    return pl.pallas_call(
        functools.partial(_combine_kernel, final=g_final is not None),
        grid_spec=grid_spec,
        out_shape=jax.ShapeDtypeStruct((n_tiles * tm, d), F32),
        compiler_params=_cparams(("arbitrary",)),
        name="moe_combine" if g_final is None else "moe_combine_final_norm",
    )(*args)


def _rope_tables(n_lat, n_ctx):
    rows = n_lat // GRID_W
    row_id, col_id = jnp.meshgrid(jnp.arange(rows), jnp.arange(GRID_W), indexing="ij")
    half = MLA_ROPE // 2
    inv_freq = ROPE_THETA ** (-jnp.arange(0, half, 2, dtype=F32) / half)

    def axis_angles(pos):
        a = pos.reshape(-1).astype(F32)[:, None] * inv_freq[None, :]
        return jnp.concatenate([a, a], axis=-1)

    ang = jnp.concatenate([axis_angles(row_id), axis_angles(col_id)], axis=-1)
    cos, sin = jnp.cos(ang), jnp.sin(ang)
    sign = jnp.where((jnp.arange(MLA_ROPE) % 32) < 16, -1.0, 1.0).astype(F32)
    sin = sin * sign[None, :]
    cos = jnp.concatenate([cos, jnp.ones((n_ctx, MLA_ROPE), F32)], axis=0)
    sin = jnp.concatenate([sin, jnp.zeros((n_ctx, MLA_ROPE), F32)], axis=0)
    return jnp.concatenate([cos, cos], axis=1), jnp.concatenate([sin, sin], axis=1)


def _split_in_proj(w):
    d = w.shape[0]
    sizes = (MLA_Q_RANK, MLA_KV_RANK, MLA_ROPE, SWA_WIDTH, SWA_KV_WIDTH, SWA_KV_WIDTH,
             CONV_WIDTH, CONV_WIDTH, CONV_WIDTH)
    offs = np.concatenate([[0], np.cumsum(sizes)])
    part = [w[:, offs[j]:offs[j + 1]] for j in range(len(sizes))]
    cq, ckv, kr, sq, sk, sv, cb, cc, cx = part
    return jnp.concatenate([cq, ckv, sq, sk, sv, cb, cc, cx, kr, jnp.zeros((d, LANES - MLA_ROPE), w.dtype)],
                           axis=1).astype(BF16)


def _split_uq(w):
    r = w.shape[0]
    w = w.reshape(r, MLA_HEADS, MLA_NOPE + MLA_ROPE)
    wn = w[:, :, :MLA_NOPE].reshape(r, MLA_HEADS * MLA_NOPE)
    wr = jnp.concatenate([w[:, :, MLA_NOPE:], jnp.zeros((r, MLA_HEADS, LANES - MLA_ROPE), w.dtype)], axis=2)
    return wn.astype(BF16), wr.reshape(r, MLA_HEADS * LANES).astype(BF16)


def _split_hi_lo(w):
    hi = w.astype(BF16)
    lo = (w - hi.astype(F32)).astype(BF16)
    return jnp.concatenate([hi, lo], axis=1)


def kernel(x, c, ctx, c_ctx, w_ada, b_ada, g_mix, w_in, g_mla_q, g_mla_kv, w_mla_uq, w_mla_ukv,
           swa_sink, conv_w, g_out, w_out, g_ffn, w_router, b_router, w_exp1, b_exp1, w_exp2,
           b_exp2, g_final):
    bsz, n_lat, d = x.shape
    n_ctx = ctx.shape[1]
    depth = w_ada.shape[0]
    assert bsz == 1 and n_lat % ATTN_K_CHUNK == 0 and n_ctx == ROW_TILE and d % LANES == 0
    t = n_lat + n_ctx
    t_pad = -(-t // ATTN_K_CHUNK) * ATTN_K_CHUNK
    n_lat_tiles = n_lat // ROW_TILE

    xs = jnp.concatenate([x[0], ctx[0]], axis=0)
    cos, sin = _rope_tables(n_lat, n_ctx)
    mods = _ada(c, c_ctx, w_ada, b_ada).reshape(depth, 2, N_MOD, d)

    n_blocks = -(-t * TOP_K // MOE_ROWS) + N_EXPERTS
    n_slots = n_blocks * MOE_ROWS
    even_sel = (jnp.arange(2 * D_EXPERT)[:, None] == 2 * jnp.arange(D_EXPERT)[None, :]).astype(BF16)

    for l in range(depth):
        mod = mods[l]
        cq, ckv, kr, sq, sk, sv, cb, cv = _inproj(xs, g_mix[l][None], mod, cos, sin,
                                                  _split_in_proj(w_in[l]), n_lat_tiles)
        wqn, wqr = _split_uq(w_mla_uq[l])
        q, k, v = _mla_up(cq, ckv, kr, g_mla_q[l][None], g_mla_kv[l][None], cos, sin,
                          wqn, wqr, w_mla_ukv[l].astype(BF16), t_pad)
        mla = _mla_attention(q, k, v, n_lat, n_ctx)
        swa = _swa(swa_sink[l], sq, sk, sv, n_lat, n_ctx)
        xs, h2, top_e, gates, rank, counts = _mix_out(
            xs, mla, swa, cb, cv, conv_w[l], g_out[l][None], w_out[l].astype(BF16), mod,
            g_ffn[l][None], _split_hi_lo(w_router[l]), b_router[l][None], n_lat_tiles)

        cnt = counts[0]
        padded = (cnt + MOE_ROWS - 1) // MOE_ROWS * MOE_ROWS
        pend = jnp.cumsum(padded).astype(jnp.int32)
        pstart = (pend - padded).astype(jnp.int32)
        bstart = jnp.arange(n_blocks, dtype=jnp.int32) * MOE_ROWS
        block_expert = jnp.minimum(jnp.sum((pend[None, :] <= bstart[:, None]).astype(jnp.int32), axis=1),
                                   N_EXPERTS - 1).astype(jnp.int32)
        n_active = (pend[-1:] // MOE_ROWS).astype(jnp.int32)
        te_flat = top_e[:, :TOP_K].reshape(-1)
        rk_flat = rank[:, :TOP_K].reshape(-1)

        xg = _dispatch(pstart, pend, te_flat, rk_flat, h2, n_slots, _slab_rows(d), _slab_pieces(d))
        y = _experts(l, block_expert, n_active, xg, w_exp1, b_exp1[:, :, None, :],
                     w_exp2, b_exp2[:, :, None, :], even_sel)
        xs = _combine(pstart, te_flat, rk_flat, xs, gates, mod, y, n_lat_tiles,
                      g_final=g_final[None] if l == depth - 1 else None)

    return xs.reshape(1, n_lat, d)
```

```python
import functools

import numpy as np
import jax
import jax.numpy as jnp
from jax import lax
from jax.experimental import pallas as pl
from jax.experimental.pallas import tpu as pltpu

F32 = jnp.float32
BF16 = jnp.bfloat16

GRID_W = 64
ROPE_THETA = 10000.0
NORM_EPS = 1e-6
N_MOD = 6
MLA_HEADS = 8
MLA_Q_RANK = 512
MLA_KV_RANK = 256
MLA_NOPE = 128
MLA_ROPE = 64
MLA_V = 128
MLA_WIDTH = MLA_HEADS * MLA_V
MLA_SCALE = (MLA_NOPE + MLA_ROPE) ** -0.5
SWA_HEADS = 8
SWA_KV_HEADS = 2
SWA_GROUP = SWA_HEADS // SWA_KV_HEADS
SWA_HEAD_DIM = 64
SWA_WINDOW = 128
SWA_WIDTH = SWA_HEADS * SWA_HEAD_DIM
SWA_KV_WIDTH = SWA_KV_HEADS * SWA_HEAD_DIM
SWA_SCALE = SWA_HEAD_DIM ** -0.5
CONV_WIDTH = 512
CONV_K = 3
MIX_WIDTH = MLA_WIDTH + SWA_WIDTH + CONV_WIDTH
N_EXPERTS = 32
TOP_K = 4
D_EXPERT = 512
SWIGLU_LIMIT = 7.0
SWIGLU_ALPHA = 1.702
LOG2E = 1.4426950408889634

LANES = 128
SUBLANES = 8
VMEM_LIMIT = 56 * 1024 * 1024

ROW_TILE = 256
SWA_TILE = 128
MOE_ROWS = 256
ATTN_Q_TILE = 1024
ATTN_K_CHUNK = 512
ATTN_UNROLL = 8
PAD_KEY_SCORE = -1e30

_O_CQ = 0
_O_CKV = _O_CQ + MLA_Q_RANK
_O_SQ = _O_CKV + MLA_KV_RANK
_O_SK = _O_SQ + SWA_WIDTH
_O_SV = _O_SK + SWA_KV_WIDTH
_O_CB = _O_SV + SWA_KV_WIDTH
_O_CC = _O_CB + CONV_WIDTH
_O_CX = _O_CC + CONV_WIDTH
_O_KR = _O_CX + CONV_WIDTH
N_IN_PAD = _O_KR + LANES


def _cparams(sem):
    return pltpu.CompilerParams(dimension_semantics=sem, vmem_limit_bytes=VMEM_LIMIT)


def _rms(x):
    return x * lax.rsqrt(jnp.mean(x * x, axis=-1, keepdims=True) + NORM_EPS)


def _slab_pieces(d):
    return d // LANES


def _slab_rows(d):
    n = _slab_pieces(d)
    return n + 4 if n % SUBLANES == 0 else n


def _store_slabs(ref, val):
    rows, d = val.shape
    sr = _slab_rows(d)
    for j in range(_slab_pieces(d)):
        ref[pl.ds(j, rows, stride=sr), :] = val[:, j * LANES:(j + 1) * LANES]


def _load_slab_piece(ref, j, rows, sr):
    return ref[pl.ds(j, rows, stride=sr), :]


def _rope(u, cos, sin_signed):
    w = u.shape[-1]
    reps = w // LANES
    if reps > 1:
        cos = jnp.concatenate([cos] * reps, axis=1)
        sin_signed = jnp.concatenate([sin_signed] * reps, axis=1)
    lane = lax.broadcasted_iota(jnp.int32, u.shape, 1)
    first = (lane % 32) < 16
    rot = jnp.where(first, pltpu.roll(u, w - 16, 1), pltpu.roll(u, 16, 1))
    return u * cos + rot * sin_signed


def _ada_kernel(s_ref, w_ref, b_ref, o_ref, *, chunk):
    d, tn = w_ref.shape

    def body(i, accs):
        a0, a1 = accs
        r0 = pl.multiple_of(i * chunk, chunk)
        w = w_ref[pl.ds(r0, chunk), :]
        s = s_ref[pl.ds(r0, chunk), :]
        s = s * jax.nn.sigmoid(s)
        p0 = (w * s[:, 0:1]).reshape(chunk // SUBLANES, SUBLANES, tn).sum(axis=0)
        p1 = (w * s[:, 1:2]).reshape(chunk // SUBLANES, SUBLANES, tn).sum(axis=0)
        return a0 + p0, a1 + p1

    z = jnp.zeros((SUBLANES, tn), F32)
    a0, a1 = lax.fori_loop(0, d // chunk, body, (z, z))
    b = b_ref[...]
    o_ref[0:1, :] = jnp.sum(a0, axis=0, keepdims=True) + b
    o_ref[1:2, :] = jnp.sum(a1, axis=0, keepdims=True) + b


def _ada(c, c_ctx, w_ada, b_ada):
    depth, d, n = w_ada.shape
    tn = 1024 if n % 1024 == 0 else 512
    chunk = 64
    s = jnp.stack([c.reshape(d), c_ctx.reshape(d)], axis=1)
    return pl.pallas_call(
        functools.partial(_ada_kernel, chunk=chunk),
        grid=(depth, n // tn),
        in_specs=[
            pl.BlockSpec((d, 2), lambda l, j: (0, 0)),
            pl.BlockSpec((None, d, tn), lambda l, j: (l, 0, j)),
            pl.BlockSpec((None, 1, tn), lambda l, j: (l, 0, j)),
        ],
        out_specs=pl.BlockSpec((None, 2, tn), lambda l, j: (l, 0, j)),
        out_shape=jax.ShapeDtypeStruct((depth, 2, n), F32),
        compiler_params=_cparams(("arbitrary", "arbitrary")),
        name="ada_mod",
    )(s, w_ada, b_ada.reshape(depth, 1, n))


def _inproj_kernel(x_ref, g_ref, mod_ref, cos_ref, sin_ref, w_ref,
                   cq_ref, ckv_ref, kr_ref, sq_ref, sk_ref, sv_ref, cb_ref, cv_ref):
    x = x_ref[...]
    h = _rms(x) * g_ref[...]
    h = h * (1.0 + mod_ref[1:2, :]) + mod_ref[0:1, :]
    hb = h.astype(BF16)
    cos = cos_ref[...]
    sin = sin_ref[...]

    def proj(a, width):
        return jnp.dot(hb, w_ref[:, a:a + width], preferred_element_type=F32)

    cq_ref[...] = proj(_O_CQ, MLA_Q_RANK)
    ckv_ref[...] = proj(_O_CKV, MLA_KV_RANK)
    kr_ref[...] = _rope(proj(_O_KR, LANES), cos, sin).astype(BF16)
    sq_ref[...] = (_rope(proj(_O_SQ, SWA_WIDTH), cos, sin) * SWA_SCALE).astype(BF16)
    sk_ref[...] = _rope(proj(_O_SK, SWA_KV_WIDTH), cos, sin).astype(BF16)
    sv_ref[...] = proj(_O_SV, SWA_KV_WIDTH).astype(BF16)
    cb_ref[...] = proj(_O_CB, CONV_WIDTH)
    cv_ref[...] = proj(_O_CC, CONV_WIDTH) * proj(_O_CX, CONV_WIDTH)


def _inproj(xs, g, mod, cos, sin, w_in_p, n_lat_tiles):
    t, d = xs.shape
    tm = ROW_TILE
    row = lambda w: pl.BlockSpec((tm, w), lambda i: (i, 0))
    outs = [
        (MLA_Q_RANK, F32), (MLA_KV_RANK, F32), (LANES, BF16), (SWA_WIDTH, BF16),
        (SWA_KV_WIDTH, BF16), (SWA_KV_WIDTH, BF16), (CONV_WIDTH, F32), (CONV_WIDTH, F32),
    ]
    return pl.pallas_call(
        _inproj_kernel,
        grid=(t // tm,),
        in_specs=[
            row(d),
            pl.BlockSpec((1, d), lambda i: (0, 0)),
            pl.BlockSpec((None, N_MOD, d), lambda i: (jnp.where(i < n_lat_tiles, 0, 1), 0, 0)),
            row(LANES), row(LANES),
            pl.BlockSpec((d, N_IN_PAD), lambda i: (0, 0)),
        ],
        out_specs=[row(w) for w, _ in outs],
        out_shape=[jax.ShapeDtypeStruct((t, w), dt) for w, dt in outs],
        compiler_params=_cparams(("arbitrary",)),
        name="mixer_in_proj",
    )(xs, g, mod, cos, sin, w_in_p)


def _mla_up_kernel(cq_ref, ckv_ref, kr_ref, gq_ref, gkv_ref, cos_ref, sin_ref,
                   wqn_ref, wqr_ref, wkv_ref, q_ref, k_ref, v_ref, *, n_real_tiles):
    i = pl.program_id(0)
    tm = cq_ref.shape[0]
    last_lane = lax.broadcasted_iota(jnp.int32, (tm, LANES), 1) == LANES - 1

    @pl.when(i < n_real_tiles)
    def _():
        hq = (_rms(cq_ref[...]) * gq_ref[...]).astype(BF16)
        qs = MLA_SCALE * LOG2E
        qn = jnp.dot(hq, wqn_ref[...], preferred_element_type=F32) * qs
        qr = _rope(jnp.dot(hq, wqr_ref[...], preferred_element_type=F32), cos_ref[...], sin_ref[...]) * qs
        hk = (_rms(ckv_ref[...]) * gkv_ref[...]).astype(BF16)
        kv = jnp.dot(hk, wkv_ref[...], preferred_element_type=F32)
        kr = kr_ref[...]
        ones = jnp.ones((tm, LANES), BF16)
        for h in range(MLA_HEADS):
            q_ref[h, :, 0:LANES] = qn[:, h * LANES:(h + 1) * LANES].astype(BF16)
            q_ref[h, :, LANES:2 * LANES] = jnp.where(last_lane, 1.0, qr[:, h * LANES:(h + 1) * LANES]).astype(BF16)
            k_ref[h, :, 0:LANES] = kv[:, h * 2 * LANES:h * 2 * LANES + LANES].astype(BF16)
            k_ref[h, :, LANES:2 * LANES] = kr
            v_ref[h, :, 0:LANES] = kv[:, h * 2 * LANES + LANES:(h + 1) * 2 * LANES].astype(BF16)
            v_ref[h, :, LANES:2 * LANES] = ones

    @pl.when(i >= n_real_tiles)
    def _():
        q_ref[...] = jnp.zeros(q_ref.shape, BF16)
        v_ref[...] = jnp.zeros(v_ref.shape, BF16)
        zero = jnp.zeros((tm, LANES), BF16)
        flag = jnp.where(last_lane, PAD_KEY_SCORE, 0.0).astype(BF16)
        for h in range(MLA_HEADS):
            k_ref[h, :, 0:LANES] = zero
            k_ref[h, :, LANES:2 * LANES] = flag


def _mla_up(cq, ckv, kr, gq, gkv, cos, sin, wqn, wqr, wkv, t_pad):
    t = cq.shape[0]
    tm = ROW_TILE
    nr = t // tm
    row = lambda w: pl.BlockSpec((tm, w), lambda i: (jnp.minimum(i, nr - 1), 0))
    full = lambda a: pl.BlockSpec(a.shape, lambda i: (0,) * a.ndim)
    hd = pl.BlockSpec((MLA_HEADS, tm, 2 * LANES), lambda i: (0, i, 0))
    shp = jax.ShapeDtypeStruct((MLA_HEADS, t_pad, 2 * LANES), BF16)
    return pl.pallas_call(
        functools.partial(_mla_up_kernel, n_real_tiles=nr),
        grid=(t_pad // tm,),
        in_specs=[row(MLA_Q_RANK), row(MLA_KV_RANK), row(LANES), full(gq), full(gkv),
                  row(LANES), row(LANES), full(wqn), full(wqr), full(wkv)],
        out_specs=[hd, hd, hd],
        out_shape=[shp, shp, shp],
        compiler_params=_cparams(("arbitrary",)),
        name="mla_up_proj",
    )(cq, ckv, kr, gq, gkv, cos, sin, wqn, wqr, wkv)


def _mla_attn_kernel(q_ref, k_ref, v_ref, o_ref, s_buf, m_sc, acc_sc, *, n_chunks, ch, unroll):
    reps = ch // LANES
    m_sc[...] = jnp.full(m_sc.shape, -jnp.inf, F32)
    acc_sc[...] = jnp.zeros(acc_sc.shape, F32)

    def scores(c, slot):
        kc = k_ref[pl.ds(pl.multiple_of(c * ch, ch), ch), :]
        s_buf[slot] = lax.dot_general(q_ref[...], kc, (((1,), (1,)), ((), ())), preferred_element_type=F32)

    def softmax_pv(c, slot):
        s = s_buf[slot]
        m_prev = m_sc[...]
        m_new = jnp.maximum(m_prev, jnp.max(s, axis=-1, keepdims=True))
        alpha = jnp.exp2(m_prev - m_new)
        p = jnp.exp2(s - jnp.concatenate([m_new] * reps, axis=1)).astype(BF16)
        vc = v_ref[pl.ds(pl.multiple_of(c * ch, ch), ch), :]
        pv = jnp.dot(p, vc, preferred_element_type=F32)
        acc_sc[...] = acc_sc[...] * jnp.concatenate([alpha, alpha], axis=1) + pv
        m_sc[...] = m_new

    scores(0, 0)

    def trip(j, carry):
        for u in range(unroll):
            scores(j * unroll + u + 1, (u + 1) % 2)
            softmax_pv(j * unroll + u, u % 2)
        return carry

    n_trips = (n_chunks - 1) // unroll
    lax.fori_loop(0, n_trips, trip, 0)
    for c in range(n_trips * unroll, n_chunks):
        if c + 1 < n_chunks:
            scores(c + 1, (c + 1) % 2)
        softmax_pv(c, c % 2)
    acc = acc_sc[...]
    o_ref[...] = acc[:, 0:MLA_V] / acc[:, MLA_V:2 * MLA_V]


def _mla_ctx_kernel(q_ref, k_ref, v_ref, prev_ref, o_ref):
    del prev_ref
    s = lax.dot_general(q_ref[...], k_ref[...], (((1,), (1,)), ((), ())), preferred_element_type=F32)
    p = jnp.exp2(s - jnp.max(s, axis=-1, keepdims=True))
    pv = jnp.dot(p.astype(BF16), v_ref[...], preferred_element_type=F32)
    o_ref[...] = pv[:, 0:MLA_V] / pv[:, MLA_V:2 * MLA_V]


def _mla_attention(q, k, v, n_lat, n_ctx):
    t = n_lat + n_ctx
    t_pad = k.shape[1]
    tq = ATTN_Q_TILE if n_lat % ATTN_Q_TILE == 0 else ROW_TILE
    ch = ATTN_K_CHUNK
    wide = 2 * LANES
    out = pl.pallas_call(
        functools.partial(_mla_attn_kernel, n_chunks=t_pad // ch, ch=ch, unroll=ATTN_UNROLL),
        grid=(MLA_HEADS, n_lat // tq),
        in_specs=[pl.BlockSpec((None, tq, wide), lambda h, i: (h, i, 0)),
                  pl.BlockSpec((None, t_pad, wide), lambda h, i: (h, 0, 0)),
                  pl.BlockSpec((None, t_pad, wide), lambda h, i: (h, 0, 0))],
        out_specs=pl.BlockSpec((tq, MLA_V), lambda h, i: (i, h)),
        out_shape=jax.ShapeDtypeStruct((t, MLA_WIDTH), F32),
        scratch_shapes=[pltpu.VMEM((2, tq, ch), F32), pltpu.VMEM((tq, LANES), F32), pltpu.VMEM((tq, wide), F32)],
        compiler_params=_cparams(("arbitrary", "arbitrary")),
        name="mla_attention",
    )(q, k, v)
    cb = n_lat // n_ctx
    return pl.pallas_call(
        _mla_ctx_kernel,
        grid=(MLA_HEADS,),
        in_specs=[pl.BlockSpec((None, n_ctx, wide), lambda h: (h, cb, 0)),
                  pl.BlockSpec((None, n_ctx, wide), lambda h: (h, cb, 0)),
                  pl.BlockSpec((None, n_ctx, wide), lambda h: (h, cb, 0)),
                  pl.BlockSpec(memory_space=pl.ANY)],
        out_specs=pl.BlockSpec((n_ctx, MLA_V), lambda h: (cb, h)),
        out_shape=jax.ShapeDtypeStruct((t, MLA_WIDTH), F32),
        input_output_aliases={3: 0},
        compiler_params=_cparams(("arbitrary",)),
        name="mla_attention_ctx",
    )(q, k, v, out)


def _swa_kernel(sink_ref, q_ref, kp_ref, ko_ref, kn_ref, kc_ref, vp_ref, vo_ref, vn_ref, vc_ref, o_ref,
                *, n_lat_tiles, n_lat, n_ctx):
    i = pl.program_id(0)
    tb = SWA_TILE
    nk = n_ctx + 3 * tb
    kall = jnp.concatenate([kc_ref[...], kp_ref[...], ko_ref[...], kn_ref[...]], axis=0)
    vall = jnp.concatenate([vc_ref[...], vp_ref[...], vo_ref[...], vn_ref[...]], axis=0)
    col = lax.broadcasted_iota(jnp.int32, (tb, nk), 1)
    r = lax.broadcasted_iota(jnp.int32, (tb, nk), 0)
    j = col - n_ctx
    kpos = (i - 1) * tb + j
    local_ok = (jnp.abs(j - tb - r) <= SWA_WINDOW) & (kpos >= 0) & (kpos < n_lat) & (i < n_lat_tiles)
    valid = (col < n_ctx) | local_ok
    lane_kv = lax.broadcasted_iota(jnp.int32, (nk, LANES), 1)
    k_roll = pltpu.roll(kall, SWA_HEAD_DIM, 1)
    v_roll = pltpu.roll(vall, SWA_HEAD_DIM, 1)
    gw = SWA_GROUP * SWA_HEAD_DIM
    lane_g = lax.broadcasted_iota(jnp.int32, (tb, gw), 1) // SWA_HEAD_DIM
    heads = [(kvh, g) for kvh in range(SWA_KV_HEADS) for g in range(SWA_GROUP)]
    lo = lane_kv < SWA_HEAD_DIM
    kts, vts, qgs = [], [], []
    for kvh in range(SWA_KV_HEADS):
        k2 = jnp.where(lo, kall, k_roll) if kvh == 0 else jnp.where(lo, k_roll, kall)
        v2 = jnp.where(lo, vall, v_roll) if kvh == 0 else jnp.where(lo, v_roll, vall)
        kts.append(jnp.concatenate([k2, k2], axis=1))
        vts.append(jnp.concatenate([v2, v2], axis=1))
        qgs.append(q_ref[:, kvh * gw:(kvh + 1) * gw])
    scores = []
    for kvh, g in heads:
        qm = jnp.where(lane_g == g, qgs[kvh], jnp.zeros_like(qgs[kvh]))
        s = lax.dot_general(qm, kts[kvh], (((1,), (1,)), ((), ())), preferred_element_type=F32)
        scores.append(jnp.where(valid, s, -jnp.inf))
    probs, inv_denoms = [], []
    for (kvh, g), s in zip(heads, scores):
        sink = sink_ref[kvh * SWA_GROUP + g]
        m = jnp.maximum(jnp.max(s, axis=-1, keepdims=True), sink)
        p = jnp.exp(s - m)
        denom = jnp.sum(p, axis=-1, keepdims=True) + jnp.exp(sink - m)
        probs.append(p.astype(BF16))
        inv_denoms.append(1.0 / denom)
    for kvh in range(SWA_KV_HEADS):
        acc = jnp.zeros((tb, gw), F32)
        for g in range(SWA_GROUP):
            h = kvh * SWA_GROUP + g
            o = jnp.dot(probs[h], vts[kvh], preferred_element_type=F32) * inv_denoms[h]
            acc = jnp.where(lane_g == g, o, acc)
        o_ref[:, kvh * gw:(kvh + 1) * gw] = acc


def _swa(sink, sq, sk, sv, n_lat, n_ctx):
    t = n_lat + n_ctx
    tb = SWA_TILE
    nt = t // tb
    nlt = n_lat // tb
    kvw = SWA_KV_WIDTH
    prev_spec = pl.BlockSpec((tb, kvw), lambda i: (jnp.maximum(i - 1, 0), 0))
    own_spec = pl.BlockSpec((tb, kvw), lambda i: (i, 0))
    next_spec = pl.BlockSpec((tb, kvw), lambda i: (jnp.minimum(i + 1, nt - 1), 0))
    ctx_spec = pl.BlockSpec((n_ctx, kvw), lambda i: (n_lat // n_ctx, 0))
    return pl.pallas_call(
        functools.partial(_swa_kernel, n_lat_tiles=nlt, n_lat=n_lat, n_ctx=n_ctx),
        grid=(nt,),
        in_specs=[pl.BlockSpec(memory_space=pltpu.SMEM),
                  pl.BlockSpec((tb, SWA_WIDTH), lambda i: (i, 0)),
                  prev_spec, own_spec, next_spec, ctx_spec,
                  prev_spec, own_spec, next_spec, ctx_spec],
        out_specs=pl.BlockSpec((tb, SWA_WIDTH), lambda i: (i, 0)),
        out_shape=jax.ShapeDtypeStruct((t, SWA_WIDTH), F32),
        compiler_params=_cparams(("arbitrary",)),
        name="swa_attention",
    )(sink, sq, sk, sk, sk, sk, sv, sv, sv, sv)


def _mix_out_kernel(x_ref, mla_ref, swa_ref, cb_ref, cv_ref, hp_ref, hn_ref, cw_ref, go_ref, wo_ref,
                    mod_ref, gf_ref, wr_ref, br_ref,
                    xo_ref, h2_ref, te_ref, gt_ref, rk_ref, cnt_ref, cnt_sc,
                    *, n_lat_tiles, n_tiles):
    i = pl.program_id(0)
    tm = x_ref.shape[0]

    @pl.when(i == 0)
    def _():
        cnt_sc[...] = jnp.zeros(cnt_sc.shape, F32)

    seg_start = (i == 0) | (i == n_lat_tiles)
    seg_end = (i == n_lat_tiles - 1) | (i == n_tiles - 1)
    v = cv_ref[...]
    rowi = lax.broadcasted_iota(jnp.int32, v.shape, 0)
    left = jnp.where(seg_start, 0.0, hp_ref[SUBLANES - 1:SUBLANES, :])
    right = jnp.where(seg_end, 0.0, hn_ref[0:1, :])
    v_dn = jnp.where(rowi == 0, left, pltpu.roll(v, 1, 0))
    v_up = jnp.where(rowi == tm - 1, right, pltpu.roll(v, tm - 1, 0))
    conv = cb_ref[...] * (v_dn * cw_ref[0:1, :] + v * cw_ref[1:2, :] + v_up * cw_ref[2:3, :])

    o1, o2 = MLA_WIDTH, MLA_WIDTH + SWA_WIDTH
    ya = (_rms(mla_ref[...]) * go_ref[:, 0:o1]).astype(BF16)
    yb = (_rms(swa_ref[...]) * go_ref[:, o1:o2]).astype(BF16)
    yc = (_rms(conv) * go_ref[:, o2:MIX_WIDTH]).astype(BF16)
    o = (jnp.dot(ya, wo_ref[0:o1, :], preferred_element_type=F32)
         + jnp.dot(yb, wo_ref[o1:o2, :], preferred_element_type=F32)
         + jnp.dot(yc, wo_ref[o2:MIX_WIDTH, :], preferred_element_type=F32))
    x = x_ref[...] + mod_ref[2:3, :] * o
    xo_ref[...] = x

    h2 = _rms(x) * gf_ref[...]
    h2 = h2 * (1.0 + mod_ref[4:5, :]) + mod_ref[3:4, :]
    _store_slabs(h2_ref, h2)
    h_hi = h2.astype(BF16)
    h_lo = (h2 - h_hi.astype(F32)).astype(BF16)
    parts = (jnp.dot(h_hi, wr_ref[...], preferred_element_type=F32)
             + jnp.dot(h_lo, wr_ref[...], preferred_element_type=F32))
    logits = parts[:, 0:N_EXPERTS] + parts[:, N_EXPERTS:2 * N_EXPERTS] + br_ref[...]
    e_iota = lax.broadcasted_iota(jnp.int32, logits.shape, 1)
    lane = lax.broadcasted_iota(jnp.int32, (tm, LANES), 1)
    work = logits
    tops, idxs = [], []
    for _ in range(TOP_K):
        m = jnp.max(work, axis=-1, keepdims=True)
        idx = jnp.min(jnp.where(work == m, e_iota, N_EXPERTS), axis=-1, keepdims=True)
        tops.append(m)
        idxs.append(idx)
        work = jnp.where(e_iota == idx, -jnp.inf, work)
    exps = [jnp.exp(tk_ - tops[0]) for tk_ in tops]
    denom = exps[0] + exps[1] + exps[2] + exps[3]

    onehots = [(e_iota == idx).astype(F32) for idx in idxs]
    sel = onehots[0] + onehots[1] + onehots[2] + onehots[3]
    rr = lax.broadcasted_iota(jnp.int32, (tm, tm), 0)
    cc = lax.broadcasted_iota(jnp.int32, (tm, tm), 1)
    tri = (rr > cc).astype(BF16)
    before = jnp.dot(tri, sel.astype(BF16), preferred_element_type=F32) + cnt_sc[...]
    cnt_new = cnt_sc[...] + jnp.sum(sel, axis=0, keepdims=True)
    cnt_sc[...] = cnt_new
    cnt_ref[...] = jnp.broadcast_to(cnt_new, cnt_ref.shape).astype(jnp.int32)

    te = jnp.zeros((tm, LANES), jnp.int32)
    rk = jnp.zeros((tm, LANES), jnp.int32)
    for k in range(TOP_K):
        rank_k = jnp.sum(before * onehots[k], axis=-1, keepdims=True).astype(jnp.int32)
        te = jnp.where(lane == k, idxs[k], te)
        rk = jnp.where(lane == k, rank_k, rk)
        gt_ref[:, k * LANES:(k + 1) * LANES] = jnp.broadcast_to(exps[k] / denom, (tm, LANES))
    te_ref[...] = te
    rk_ref[...] = rk


def _mix_out(xs, mla, swa, cb, cv, conv_w, g_out, w_out_b, mod, g_ffn, w_r, b_r, n_lat_tiles):
    t, d = xs.shape
    tm = ROW_TILE
    nt = t // tm
    hb = tm // SUBLANES
    row = lambda w: pl.BlockSpec((tm, w), lambda i: (i, 0))
    full = lambda a: pl.BlockSpec(a.shape, lambda i: (0,) * a.ndim)
    return pl.pallas_call(
        functools.partial(_mix_out_kernel, n_lat_tiles=n_lat_tiles, n_tiles=nt),
        grid=(nt,),
        in_specs=[row(d), row(MLA_WIDTH), row(SWA_WIDTH), row(CONV_WIDTH), row(CONV_WIDTH),
                  pl.BlockSpec((SUBLANES, CONV_WIDTH), lambda i: (jnp.maximum(i * hb - 1, 0), 0)),
                  pl.BlockSpec((SUBLANES, CONV_WIDTH), lambda i: (jnp.minimum((i + 1) * hb, nt * hb - 1), 0)),
                  full(conv_w), full(g_out), full(w_out_b),
                  pl.BlockSpec((None, N_MOD, d), lambda i: (jnp.where(i < n_lat_tiles, 0, 1), 0, 0)),
                  full(g_ffn), full(w_r), full(b_r)],
        out_specs=[row(d), pl.BlockSpec((tm * _slab_rows(d), LANES), lambda i: (i, 0)),
                   row(LANES), row(TOP_K * LANES), row(LANES),
                   pl.BlockSpec((SUBLANES, N_EXPERTS), lambda i: (0, 0))],
        out_shape=[jax.ShapeDtypeStruct((t, d), F32),
                   jax.ShapeDtypeStruct((t * _slab_rows(d), LANES), F32),
                   jax.ShapeDtypeStruct((t, LANES), jnp.int32), jax.ShapeDtypeStruct((t, TOP_K * LANES), F32),
                   jax.ShapeDtypeStruct((t, LANES), jnp.int32),
                   jax.ShapeDtypeStruct((SUBLANES, N_EXPERTS), jnp.int32)],
        scratch_shapes=[pltpu.VMEM((1, N_EXPERTS), F32)],
        compiler_params=_cparams(("arbitrary",)),
        name="mix_out_ffn_route",
    )(xs, mla, swa, cb, cv, cv, cv, conv_w, g_out, w_out_b, mod, g_ffn, w_r, b_r)


def _dispatch_kernel(pstart_ref, pend_ref, te_ref, rk_ref, h_ref, xs_ref, zero_buf, sem, zsem, *, sr, nr):
    tm = h_ref.shape[0] // sr
    bm = zero_buf.shape[0] // sr

    @pl.when(pl.program_id(0) == 0)
    def _():
        zero_buf[...] = jnp.zeros(zero_buf.shape, F32)

        def tail_copy(e):
            last = pl.multiple_of(jnp.maximum(pend_ref[e] - bm, 0) * sr, bm * sr)
            return pltpu.make_async_copy(zero_buf, xs_ref.at[pl.ds(last, bm * sr), :], zsem)

        def zstart(e, c):
            @pl.when(pend_ref[e] > pstart_ref[e])
            def _():
                tail_copy(e).start()
            return c

        def zwait(e, c):
            @pl.when(pend_ref[e] > pstart_ref[e])
            def _():
                tail_copy(e).wait()
            return c

        lax.fori_loop(0, N_EXPERTS, zstart, 0)
        lax.fori_loop(0, N_EXPERTS, zwait, 0)

    def row_copy(r, k):
        a = r * TOP_K + k
        dest = pl.multiple_of((pstart_ref[te_ref[a]] + rk_ref[a]) * sr, sr)
        return pltpu.make_async_copy(h_ref.at[pl.ds(pl.multiple_of(r * sr, sr), nr), :],
                                     xs_ref.at[pl.ds(dest, nr), :], sem)

    def issue(r, c):
        for k in range(TOP_K):
            row_copy(r, k).start()
        return c

    def drain(r, c):
        for k in range(TOP_K):
            row_copy(r, k).wait()
        return c

    lax.fori_loop(0, tm, issue, 0)
    lax.fori_loop(0, tm, drain, 0)


def _dispatch(pstart, pend, te_flat, rk_flat, h2, n_slots, sr, nr):
    w = h2.shape[1]
    tm = ROW_TILE
    flat = pl.BlockSpec((tm * TOP_K,), lambda i, ps, pe: (i,), memory_space=pltpu.SMEM)
    grid_spec = pltpu.PrefetchScalarGridSpec(
        num_scalar_prefetch=2,
        grid=(h2.shape[0] // (tm * sr),),
        in_specs=[flat, flat, pl.BlockSpec((tm * sr, w), lambda i, ps, pe: (i, 0))],
        out_specs=pl.BlockSpec(memory_space=pl.ANY),
        scratch_shapes=[pltpu.VMEM((MOE_ROWS * sr, w), F32), pltpu.SemaphoreType.DMA(()),
                        pltpu.SemaphoreType.DMA(())],
    )
    return pl.pallas_call(
        functools.partial(_dispatch_kernel, sr=sr, nr=nr),
        grid_spec=grid_spec,
        out_shape=jax.ShapeDtypeStruct((n_slots * sr, w), F32),
        compiler_params=_cparams(("arbitrary",)),
        name="moe_dispatch",
    )(pstart, pend, te_flat, rk_flat, h2)


def _expert_kernel(be_ref, na_ref, x_ref, w1_ref, b1_ref, w2_ref, b2_ref, sel_ref, y_ref, w1b, w2b):
    b = pl.program_id(0)
    active = b < na_ref[0]
    new_expert = (b == 0) | (be_ref[b] != be_ref[jnp.maximum(b - 1, 0)])

    @pl.when(active & new_expert)
    def _():
        w1b[...] = w1_ref[...].astype(BF16)
        w2b[...] = w2_ref[...].astype(BF16)

    @pl.when(active)
    def _():
        sr = _slab_rows(w1b.shape[0])
        bm = x_ref.shape[0] // sr
        gu = b1_ref[...]
        for j in range(0, _slab_pieces(w1b.shape[0]), 2):
            xj = jnp.concatenate([_load_slab_piece(x_ref, j, bm, sr), _load_slab_piece(x_ref, j + 1, bm, sr)],
                                 axis=1).astype(BF16)
            gu = gu + jnp.dot(xj, w1b[j * LANES:(j + 2) * LANES, :], preferred_element_type=F32)
        gate = jnp.minimum(gu, SWIGLU_LIMIT)
        a = gate * jax.nn.sigmoid(SWIGLU_ALPHA * gate)
        u = jnp.clip(gu, -SWIGLU_LIMIT, SWIGLU_LIMIT) + 1.0
        act = a * pltpu.roll(u, u.shape[1] - 1, 1)
        act = jnp.dot(act.astype(BF16), sel_ref[...], preferred_element_type=F32)
        y = jnp.dot(act.astype(BF16), w2b[...], preferred_element_type=F32) + b2_ref[...]
        _store_slabs(y_ref, y)


def _experts(layer, block_expert, n_active, xs, w1, b1, w2, b2, sel):
    d = w1.shape[2]
    sr = _slab_rows(d)
    w = xs.shape[1]
    bm = MOE_ROWS
    nb = xs.shape[0] // (bm * sr)
    blk = lambda b, be, na: (jnp.minimum(b, na[0] - 1), 0)
    wsel = lambda b, be, na: (layer, be[jnp.minimum(b, na[0] - 1)], 0, 0)
    grid_spec = pltpu.PrefetchScalarGridSpec(
        num_scalar_prefetch=2,
        grid=(nb,),
        in_specs=[pl.BlockSpec((bm * sr, w), blk),
                  pl.BlockSpec((None, None, d, 2 * D_EXPERT), wsel),
                  pl.BlockSpec((None, None, 1, 2 * D_EXPERT), wsel),
                  pl.BlockSpec((None, None, D_EXPERT, d), wsel),
                  pl.BlockSpec((None, None, 1, d), wsel),
                  pl.BlockSpec((2 * D_EXPERT, D_EXPERT), lambda b, be, na: (0, 0))],
        out_specs=pl.BlockSpec((bm * sr, w), blk),
        scratch_shapes=[pltpu.VMEM((d, 2 * D_EXPERT), BF16), pltpu.VMEM((D_EXPERT, d), BF16)],
    )
    return pl.pallas_call(
        _expert_kernel,
        grid_spec=grid_spec,
        out_shape=jax.ShapeDtypeStruct(xs.shape, F32),
        compiler_params=_cparams(("arbitrary",)),
        name="moe_experts",
    )(block_expert, n_active, xs, w1, b1, w2, b2, sel)


def _combine_kernel(pstart_ref, te_ref, rk_ref, x_ref, gt_ref, mod_ref, y_ref, *rest, final):
    if final:
        gfin_ref, o_ref, buf, sem = rest
    else:
        o_ref, buf, sem = rest
    tm, d = x_ref.shape
    sr = _slab_rows(d)
    nr = _slab_pieces(d)

    def row_copy(r, k):
        a = r * TOP_K + k
        src = pl.multiple_of((pstart_ref[te_ref[a]] + rk_ref[a]) * sr, sr)
        return pltpu.make_async_copy(y_ref.at[pl.ds(src, nr), :],
                                     buf.at[k, pl.ds(pl.multiple_of(r * sr, sr), nr), :], sem)

    def issue(r, c):
        for k in range(TOP_K):
            row_copy(r, k).start()
        return c

    def drain(r, c):
        for k in range(TOP_K):
            row_copy(r, k).wait()
        return c

    lax.fori_loop(0, tm, issue, 0)
    lax.fori_loop(0, tm, drain, 0)
    for j in range(nr):
        f = gt_ref[:, 0:LANES] * _load_slab_piece(buf.at[0], j, tm, sr)
        for k in range(1, TOP_K):
            f = f + gt_ref[:, k * LANES:(k + 1) * LANES] * _load_slab_piece(buf.at[k], j, tm, sr)
        cols = slice(j * LANES, (j + 1) * LANES)
        o_ref[:, cols] = x_ref[:, cols] + mod_ref[5:6, cols] * f
    if final:
        o_ref[...] = _rms(o_ref[...]) * gfin_ref[...]


def _combine(pstart, te_flat, rk_flat, xs, gates, mod, y, n_lat_tiles, g_final=None):
    t, d = xs.shape
    tm = ROW_TILE
    n_tiles = t // tm if g_final is None else n_lat_tiles
    flat = pl.BlockSpec((tm * TOP_K,), lambda i, ps: (i,), memory_space=pltpu.SMEM)
    in_specs = [flat, flat,
                pl.BlockSpec((tm, d), lambda i, ps: (i, 0)),
                pl.BlockSpec((tm, TOP_K * LANES), lambda i, ps: (i, 0)),
                pl.BlockSpec((None, N_MOD, d), lambda i, ps: (jnp.where(i < n_lat_tiles, 0, 1), 0, 0)),
                pl.BlockSpec(memory_space=pl.ANY)]
    args = [pstart, te_flat, rk_flat, xs, gates, mod, y]
    if g_final is not None:
        in_specs.append(pl.BlockSpec((1, d), lambda i, ps: (0, 0)))
        args.append(g_final)
    grid_spec = pltpu.PrefetchScalarGridSpec(
        num_scalar_prefetch=1,
        grid=(n_tiles,),
        in_specs=in_specs,
        out_specs=pl.BlockSpec((tm, d), lambda i, ps: (i, 0)),
        scratch_shapes=[pltpu.VMEM((TOP_K, tm * _slab_rows(d), LANES), F32), pltpu.SemaphoreType.DMA(())],
    )
    return pl.pallas_call(
        functools.partial(_combine_kernel, final=g_final is not None),
        grid_spec=grid_spec,
        out_shape=jax.ShapeDtypeStruct((n_tiles * tm, d), F32),
        compiler_params=_cparams(("arbitrary",)),
        name="moe_combine" if g_final is None else "moe_combine_final_norm",
    )(*args)


def _rope_tables(n_lat, n_ctx):
    rows = n_lat // GRID_W
    row_id, col_id = jnp.meshgrid(jnp.arange(rows), jnp.arange(GRID_W), indexing="ij")
    half = MLA_ROPE // 2
    inv_freq = ROPE_THETA ** (-jnp.arange(0, half, 2, dtype=F32) / half)

    def axis_angles(pos):
        a = pos.reshape(-1).astype(F32)[:, None] * inv_freq[None, :]
        return jnp.concatenate([a, a], axis=-1)

    ang = jnp.concatenate([axis_angles(row_id), axis_angles(col_id)], axis=-1)
    cos, sin = jnp.cos(ang), jnp.sin(ang)
    sign = jnp.where((jnp.arange(MLA_ROPE) % 32) < 16, -1.0, 1.0).astype(F32)
    sin = sin * sign[None, :]
    cos = jnp.concatenate([cos, jnp.ones((n_ctx, MLA_ROPE), F32)], axis=0)
    sin = jnp.concatenate([sin, jnp.zeros((n_ctx, MLA_ROPE), F32)], axis=0)
    return jnp.concatenate([cos, cos], axis=1), jnp.concatenate([sin, sin], axis=1)


def _split_in_proj(w):
    d = w.shape[0]
    sizes = (MLA_Q_RANK, MLA_KV_RANK, MLA_ROPE, SWA_WIDTH, SWA_KV_WIDTH, SWA_KV_WIDTH,
             CONV_WIDTH, CONV_WIDTH, CONV_WIDTH)
    offs = np.concatenate([[0], np.cumsum(sizes)])
    part = [w[:, offs[j]:offs[j + 1]] for j in range(len(sizes))]
    cq, ckv, kr, sq, sk, sv, cb, cc, cx = part
    return jnp.concatenate([cq, ckv, sq, sk, sv, cb, cc, cx, kr, jnp.zeros((d, LANES - MLA_ROPE), w.dtype)],
                           axis=1).astype(BF16)


def _split_uq(w):
    r = w.shape[0]
    w = w.reshape(r, MLA_HEADS, MLA_NOPE + MLA_ROPE)
    wn = w[:, :, :MLA_NOPE].reshape(r, MLA_HEADS * MLA_NOPE)
    wr = jnp.concatenate([w[:, :, MLA_NOPE:], jnp.zeros((r, MLA_HEADS, LANES - MLA_ROPE), w.dtype)], axis=2)
    return wn.astype(BF16), wr.reshape(r, MLA_HEADS * LANES).astype(BF16)


def _split_hi_lo(w):
    hi = w.astype(BF16)
    lo = (w - hi.astype(F32)).astype(BF16)
    return jnp.concatenate([hi, lo], axis=1)


def kernel(x, c, ctx, c_ctx, w_ada, b_ada, g_mix, w_in, g_mla_q, g_mla_kv, w_mla_uq, w_mla_ukv,
           swa_sink, conv_w, g_out, w_out, g_ffn, w_router, b_router, w_exp1, b_exp1, w_exp2,
           b_exp2, g_final):
    bsz, n_lat, d = x.shape
    n_ctx = ctx.shape[1]
    depth = w_ada.shape[0]
    assert bsz == 1 and n_lat % ATTN_K_CHUNK == 0 and n_ctx == ROW_TILE and d % LANES == 0
    t = n_lat + n_ctx
    t_pad = -(-t // ATTN_K_CHUNK) * ATTN_K_CHUNK
    n_lat_tiles = n_lat // ROW_TILE

    xs = jnp.concatenate([x[0], ctx[0]], axis=0)
    cos, sin = _rope_tables(n_lat, n_ctx)
    mods = _ada(c, c_ctx, w_ada, b_ada).reshape(depth, 2, N_MOD, d)

    n_blocks = -(-t * TOP_K // MOE_ROWS) + N_EXPERTS
    n_slots = n_blocks * MOE_ROWS
    even_sel = (jnp.arange(2 * D_EXPERT)[:, None] == 2 * jnp.arange(D_EXPERT)[None, :]).astype(BF16)

    for l in range(depth):
        mod = mods[l]
        cq, ckv, kr, sq, sk, sv, cb, cv = _inproj(xs, g_mix[l][None], mod, cos, sin,
                                                  _split_in_proj(w_in[l]), n_lat_tiles)
        wqn, wqr = _split_uq(w_mla_uq[l])
        q, k, v = _mla_up(cq, ckv, kr, g_mla_q[l][None], g_mla_kv[l][None], cos, sin,
                          wqn, wqr, w_mla_ukv[l].astype(BF16), t_pad)
        mla = _mla_attention(q, k, v, n_lat, n_ctx)
        swa = _swa(swa_sink[l], sq, sk, sv, n_lat, n_ctx)
        xs, h2, top_e, gates, rank, counts = _mix_out(
            xs, mla, swa, cb, cv, conv_w[l], g_out[l][None], w_out[l].astype(BF16), mod,
            g_ffn[l][None], _split_hi_lo(w_router[l]), b_router[l][None], n_lat_tiles)

        cnt = counts[0]
        padded = (cnt + MOE_ROWS - 1) // MOE_ROWS * MOE_ROWS
        pend = jnp.cumsum(padded).astype(jnp.int32)
        pstart = (pend - padded).astype(jnp.int32)
        bstart = jnp.arange(n_blocks, dtype=jnp.int32) * MOE_ROWS
        block_expert = jnp.minimum(jnp.sum((pend[None, :] <= bstart[:, None]).astype(jnp.int32), axis=1),
                                   N_EXPERTS - 1).astype(jnp.int32)
        n_active = (pend[-1:] // MOE_ROWS).astype(jnp.int32)
        te_flat = top_e[:, :TOP_K].reshape(-1)
        rk_flat = rank[:, :TOP_K].reshape(-1)

        xg = _dispatch(pstart, pend, te_flat, rk_flat, h2, n_slots, _slab_rows(d), _slab_pieces(d))
        y = _experts(l, block_expert, n_active, xg, w_exp1, b_exp1[:, :, None, :],
                     w_exp2, b_exp2[:, :, None, :], even_sel)
        xs = _combine(pstart, te_flat, rk_flat, xs, gates, mod, y, n_lat_tiles,
                      g_final=g_final[None] if l == depth - 1 else None)

    return xs.reshape(1, n_lat, d)
```

```python
import functools

import numpy as np
import jax
import jax.numpy as jnp
from jax import lax
from jax.experimental import pallas as pl
from jax.experimental.pallas import tpu as pltpu

F32 = jnp.float32
BF16 = jnp.bfloat16

GRID_W = 64
ROPE_THETA = 10000.0
NORM_EPS = 1e-6
N_MOD = 6
MLA_HEADS = 8
MLA_Q_RANK = 512
MLA_KV_RANK = 256
MLA_NOPE = 128
MLA_ROPE = 64
MLA_V = 128
MLA_WIDTH = MLA_HEADS * MLA_V
MLA_SCALE = (MLA_NOPE + MLA_ROPE) ** -0.5
SWA_HEADS = 8
SWA_KV_HEADS = 2
SWA_GROUP = SWA_HEADS // SWA_KV_HEADS
SWA_HEAD_DIM = 64
SWA_WINDOW = 128
SWA_WIDTH = SWA_HEADS * SWA_HEAD_DIM
SWA_KV_WIDTH = SWA_KV_HEADS * SWA_HEAD_DIM
SWA_SCALE = SWA_HEAD_DIM ** -0.5
CONV_WIDTH = 512
CONV_K = 3
MIX_WIDTH = MLA_WIDTH + SWA_WIDTH + CONV_WIDTH
N_EXPERTS = 32
TOP_K = 4
D_EXPERT = 512
SWIGLU_LIMIT = 7.0
SWIGLU_ALPHA = 1.702
LOG2E = 1.4426950408889634

LANES = 128
SUBLANES = 8
VMEM_LIMIT = 56 * 1024 * 1024

ROW_TILE = 256
SWA_TILE = 128
MOE_ROWS = 256
ATTN_Q_TILE = 1024
ATTN_K_CHUNK = 512
ATTN_UNROLL = 8
PAD_KEY_SCORE = -1e30

_O_CQ = 0
_O_CKV = _O_CQ + MLA_Q_RANK
_O_SQ = _O_CKV + MLA_KV_RANK
_O_SK = _O_SQ + SWA_WIDTH
_O_SV = _O_SK + SWA_KV_WIDTH
_O_CB = _O_SV + SWA_KV_WIDTH
_O_CC = _O_CB + CONV_WIDTH
_O_CX = _O_CC + CONV_WIDTH
_O_KR = _O_CX + CONV_WIDTH
N_IN_PAD = _O_KR + LANES


def _cparams(sem):
    return pltpu.CompilerParams(dimension_semantics=sem, vmem_limit_bytes=VMEM_LIMIT)


def _rms(x):
    return x * lax.rsqrt(jnp.mean(x * x, axis=-1, keepdims=True) + NORM_EPS)


U32 = jnp.uint32


def _slab_pieces(d):
    return d // (2 * LANES)


def _slab_rows(d):
    n = _slab_pieces(d)
    return n + 4 if n % SUBLANES == 0 else n


def _pack_pair(a, b):
    ah = lax.bitcast_convert_type(a.astype(BF16).astype(F32), U32)
    bh = lax.bitcast_convert_type(b.astype(BF16).astype(F32), U32)
    return ah | (bh >> 16)


def _unpack_pair(u):
    a = lax.bitcast_convert_type(u & U32(0xFFFF0000), F32)
    b = lax.bitcast_convert_type(u << 16, F32)
    return a, b


def _store_slabs(ref, val):
    rows, d = val.shape
    sr = _slab_rows(d)
    for j in range(_slab_pieces(d)):
        lo = 2 * j * LANES
        ref[pl.ds(j, rows, stride=sr), :] = _pack_pair(val[:, lo:lo + LANES], val[:, lo + LANES:lo + 2 * LANES])


def _load_slab_piece(ref, j, rows, sr):
    a, b = _unpack_pair(ref[pl.ds(j, rows, stride=sr), :])
    return jnp.concatenate([a, b], axis=1)


def _rope(u, cos, sin_signed):
    w = u.shape[-1]
    reps = w // LANES
    if reps > 1:
        cos = jnp.concatenate([cos] * reps, axis=1)
        sin_signed = jnp.concatenate([sin_signed] * reps, axis=1)
    lane = lax.broadcasted_iota(jnp.int32, u.shape, 1)
    first = (lane % 32) < 16
    rot = jnp.where(first, pltpu.roll(u, w - 16, 1), pltpu.roll(u, 16, 1))
    return u * cos + rot * sin_signed


def _ada_kernel(s_ref, w_ref, b_ref, o_ref, *, chunk):
    d, tn = w_ref.shape

    def body(i, accs):
        a0, a1 = accs
        r0 = pl.multiple_of(i * chunk, chunk)
        w = w_ref[pl.ds(r0, chunk), :]
        s = s_ref[pl.ds(r0, chunk), :]
        s = s * jax.nn.sigmoid(s)
        p0 = (w * s[:, 0:1]).reshape(chunk // SUBLANES, SUBLANES, tn).sum(axis=0)
        p1 = (w * s[:, 1:2]).reshape(chunk // SUBLANES, SUBLANES, tn).sum(axis=0)
        return a0 + p0, a1 + p1

    z = jnp.zeros((SUBLANES, tn), F32)
    a0, a1 = lax.fori_loop(0, d // chunk, body, (z, z))
    b = b_ref[...]
    o_ref[0:1, :] = jnp.sum(a0, axis=0, keepdims=True) + b
    o_ref[1:2, :] = jnp.sum(a1, axis=0, keepdims=True) + b


def _ada(c, c_ctx, w_ada, b_ada):
    depth, d, n = w_ada.shape
    tn = 1024 if n % 1024 == 0 else 512
    chunk = 64
    s = jnp.stack([c.reshape(d), c_ctx.reshape(d)], axis=1)
    return pl.pallas_call(
        functools.partial(_ada_kernel, chunk=chunk),
        grid=(depth, n // tn),
        in_specs=[
            pl.BlockSpec((d, 2), lambda l, j: (0, 0)),
            pl.BlockSpec((None, d, tn), lambda l, j: (l, 0, j)),
            pl.BlockSpec((None, 1, tn), lambda l, j: (l, 0, j)),
        ],
        out_specs=pl.BlockSpec((None, 2, tn), lambda l, j: (l, 0, j)),
        out_shape=jax.ShapeDtypeStruct((depth, 2, n), F32),
        compiler_params=_cparams(("arbitrary", "arbitrary")),
        name="ada_mod",
    )(s, w_ada, b_ada.reshape(depth, 1, n))


def _inproj_kernel(x_ref, g_ref, mod_ref, cos_ref, sin_ref, w_ref,
                   cq_ref, ckv_ref, kr_ref, sq_ref, sk_ref, sv_ref, cb_ref, cv_ref):
    x = x_ref[...]
    h = _rms(x) * g_ref[...]
    h = h * (1.0 + mod_ref[1:2, :]) + mod_ref[0:1, :]
    hb = h.astype(BF16)
    cos = cos_ref[...]
    sin = sin_ref[...]

    def proj(a, width):
        return jnp.dot(hb, w_ref[:, a:a + width], preferred_element_type=F32)

    cq_ref[...] = proj(_O_CQ, MLA_Q_RANK)
    ckv_ref[...] = proj(_O_CKV, MLA_KV_RANK)
    kr_ref[...] = _rope(proj(_O_KR, LANES), cos, sin).astype(BF16)
    sq_ref[...] = (_rope(proj(_O_SQ, SWA_WIDTH), cos, sin) * SWA_SCALE).astype(BF16)
    sk_ref[...] = _rope(proj(_O_SK, SWA_KV_WIDTH), cos, sin).astype(BF16)
    sv_ref[...] = proj(_O_SV, SWA_KV_WIDTH).astype(BF16)
    cb_ref[...] = proj(_O_CB, CONV_WIDTH)
    cv_ref[...] = proj(_O_CC, CONV_WIDTH) * proj(_O_CX, CONV_WIDTH)


def _inproj(xs, g, mod, cos, sin, w_in_p, n_lat_tiles):
    t, d = xs.shape
    tm = ROW_TILE
    row = lambda w: pl.BlockSpec((tm, w), lambda i: (i, 0))
    outs = [
        (MLA_Q_RANK, F32), (MLA_KV_RANK, F32), (LANES, BF16), (SWA_WIDTH, BF16),
        (SWA_KV_WIDTH, BF16), (SWA_KV_WIDTH, BF16), (CONV_WIDTH, F32), (CONV_WIDTH, F32),
    ]
    return pl.pallas_call(
        _inproj_kernel,
        grid=(t // tm,),
        in_specs=[
            row(d),
            pl.BlockSpec((1, d), lambda i: (0, 0)),
            pl.BlockSpec((None, N_MOD, d), lambda i: (jnp.where(i < n_lat_tiles, 0, 1), 0, 0)),
            row(LANES), row(LANES),
            pl.BlockSpec((d, N_IN_PAD), lambda i: (0, 0)),
        ],
        out_specs=[row(w) for w, _ in outs],
        out_shape=[jax.ShapeDtypeStruct((t, w), dt) for w, dt in outs],
        compiler_params=_cparams(("arbitrary",)),
        name="mixer_in_proj",
    )(xs, g, mod, cos, sin, w_in_p)


def _mla_up_kernel(cq_ref, ckv_ref, kr_ref, gq_ref, gkv_ref, cos_ref, sin_ref,
                   wqn_ref, wqr_ref, wkv_ref, q_ref, k_ref, v_ref, *, n_real_tiles):
    i = pl.program_id(0)
    tm = cq_ref.shape[0]
    last_lane = lax.broadcasted_iota(jnp.int32, (tm, LANES), 1) == LANES - 1

    @pl.when(i < n_real_tiles)
    def _():
        hq = (_rms(cq_ref[...]) * gq_ref[...]).astype(BF16)
        qs = MLA_SCALE * LOG2E
        qn = jnp.dot(hq, wqn_ref[...], preferred_element_type=F32) * qs
        qr = _rope(jnp.dot(hq, wqr_ref[...], preferred_element_type=F32), cos_ref[...], sin_ref[...]) * qs
        hk = (_rms(ckv_ref[...]) * gkv_ref[...]).astype(BF16)
        kv = jnp.dot(hk, wkv_ref[...], preferred_element_type=F32)
        kr = kr_ref[...]
        ones = jnp.ones((tm, LANES), BF16)
        for h in range(MLA_HEADS):
            q_ref[h, :, 0:LANES] = qn[:, h * LANES:(h + 1) * LANES].astype(BF16)
            q_ref[h, :, LANES:2 * LANES] = jnp.where(last_lane, 1.0, qr[:, h * LANES:(h + 1) * LANES]).astype(BF16)
            k_ref[h, :, 0:LANES] = kv[:, h * 2 * LANES:h * 2 * LANES + LANES].astype(BF16)
            k_ref[h, :, LANES:2 * LANES] = kr
            v_ref[h, :, 0:LANES] = kv[:, h * 2 * LANES + LANES:(h + 1) * 2 * LANES].astype(BF16)
            v_ref[h, :, LANES:2 * LANES] = ones

    @pl.when(i >= n_real_tiles)
    def _():
        q_ref[...] = jnp.zeros(q_ref.shape, BF16)
        v_ref[...] = jnp.zeros(v_ref.shape, BF16)
        zero = jnp.zeros((tm, LANES), BF16)
        flag = jnp.where(last_lane, PAD_KEY_SCORE, 0.0).astype(BF16)
        for h in range(MLA_HEADS):
            k_ref[h, :, 0:LANES] = zero
            k_ref[h, :, LANES:2 * LANES] = flag


def _mla_up(cq, ckv, kr, gq, gkv, cos, sin, wqn, wqr, wkv, t_pad):
    t = cq.shape[0]
    tm = ROW_TILE
    nr = t // tm
    row = lambda w: pl.BlockSpec((tm, w), lambda i: (jnp.minimum(i, nr - 1), 0))
    full = lambda a: pl.BlockSpec(a.shape, lambda i: (0,) * a.ndim)
    hd = pl.BlockSpec((MLA_HEADS, tm, 2 * LANES), lambda i: (0, i, 0))
    shp = jax.ShapeDtypeStruct((MLA_HEADS, t_pad, 2 * LANES), BF16)
    return pl.pallas_call(
        functools.partial(_mla_up_kernel, n_real_tiles=nr),
        grid=(t_pad // tm,),
        in_specs=[row(MLA_Q_RANK), row(MLA_KV_RANK), row(LANES), full(gq), full(gkv),
                  row(LANES), row(LANES), full(wqn), full(wqr), full(wkv)],
        out_specs=[hd, hd, hd],
        out_shape=[shp, shp, shp],
        compiler_params=_cparams(("arbitrary",)),
        name="mla_up_proj",
    )(cq, ckv, kr, gq, gkv, cos, sin, wqn, wqr, wkv)


def _mla_attn_kernel(q_ref, k_ref, v_ref, o_ref, s_buf, m_sc, acc_sc, *, n_chunks, ch, unroll):
    reps = ch // LANES
    m_sc[...] = jnp.full(m_sc.shape, -jnp.inf, F32)
    acc_sc[...] = jnp.zeros(acc_sc.shape, F32)

    def scores(c, slot):
        kc = k_ref[pl.ds(pl.multiple_of(c * ch, ch), ch), :]
        s_buf[slot] = lax.dot_general(q_ref[...], kc, (((1,), (1,)), ((), ())), preferred_element_type=F32)

    def softmax_pv(c, slot):
        s = s_buf[slot]
        m_prev = m_sc[...]
        m_new = jnp.maximum(m_prev, jnp.max(s, axis=-1, keepdims=True))
        alpha = jnp.exp2(m_prev - m_new)
        p = jnp.exp2(s - jnp.concatenate([m_new] * reps, axis=1)).astype(BF16)
        vc = v_ref[pl.ds(pl.multiple_of(c * ch, ch), ch), :]
        pv = jnp.dot(p, vc, preferred_element_type=F32)
        acc_sc[...] = acc_sc[...] * jnp.concatenate([alpha, alpha], axis=1) + pv
        m_sc[...] = m_new

    scores(0, 0)

    def trip(j, carry):
        for u in range(unroll):
            scores(j * unroll + u + 1, (u + 1) % 2)
            softmax_pv(j * unroll + u, u % 2)
        return carry

    n_trips = (n_chunks - 1) // unroll
    lax.fori_loop(0, n_trips, trip, 0)
    for c in range(n_trips * unroll, n_chunks):
        if c + 1 < n_chunks:
            scores(c + 1, (c + 1) % 2)
        softmax_pv(c, c % 2)
    acc = acc_sc[...]
    o_ref[...] = acc[:, 0:MLA_V] / acc[:, MLA_V:2 * MLA_V]


def _mla_ctx_kernel(q_ref, k_ref, v_ref, prev_ref, o_ref):
    del prev_ref
    s = lax.dot_general(q_ref[...], k_ref[...], (((1,), (1,)), ((), ())), preferred_element_type=F32)
    p = jnp.exp2(s - jnp.max(s, axis=-1, keepdims=True))
    pv = jnp.dot(p.astype(BF16), v_ref[...], preferred_element_type=F32)
    o_ref[...] = pv[:, 0:MLA_V] / pv[:, MLA_V:2 * MLA_V]


def _mla_attention(q, k, v, n_lat, n_ctx):
    t = n_lat + n_ctx
    t_pad = k.shape[1]
    tq = ATTN_Q_TILE if n_lat % ATTN_Q_TILE == 0 else ROW_TILE
    ch = ATTN_K_CHUNK
    wide = 2 * LANES
    out = pl.pallas_call(
        functools.partial(_mla_attn_kernel, n_chunks=t_pad // ch, ch=ch, unroll=ATTN_UNROLL),
        grid=(MLA_HEADS, n_lat // tq),
        in_specs=[pl.BlockSpec((None, tq, wide), lambda h, i: (h, i, 0)),
                  pl.BlockSpec((None, t_pad, wide), lambda h, i: (h, 0, 0)),
                  pl.BlockSpec((None, t_pad, wide), lambda h, i: (h, 0, 0))],
        out_specs=pl.BlockSpec((tq, MLA_V), lambda h, i: (i, h)),
        out_shape=jax.ShapeDtypeStruct((t, MLA_WIDTH), F32),
        scratch_shapes=[pltpu.VMEM((2, tq, ch), F32), pltpu.VMEM((tq, LANES), F32), pltpu.VMEM((tq, wide), F32)],
        compiler_params=_cparams(("arbitrary", "arbitrary")),
        name="mla_attention",
    )(q, k, v)
    cb = n_lat // n_ctx
    return pl.pallas_call(
        _mla_ctx_kernel,
        grid=(MLA_HEADS,),
        in_specs=[pl.BlockSpec((None, n_ctx, wide), lambda h: (h, cb, 0)),
                  pl.BlockSpec((None, n_ctx, wide), lambda h: (h, cb, 0)),
                  pl.BlockSpec((None, n_ctx, wide), lambda h: (h, cb, 0)),
                  pl.BlockSpec(memory_space=pl.ANY)],
        out_specs=pl.BlockSpec((n_ctx, MLA_V), lambda h: (cb, h)),
        out_shape=jax.ShapeDtypeStruct((t, MLA_WIDTH), F32),
        input_output_aliases={3: 0},
        compiler_params=_cparams(("arbitrary",)),
        name="mla_attention_ctx",
    )(q, k, v, out)


def _swa_kernel(sink_ref, q_ref, kp_ref, ko_ref, kn_ref, kc_ref, vp_ref, vo_ref, vn_ref, vc_ref, o_ref,
                *, n_lat_tiles, n_lat, n_ctx):
    i = pl.program_id(0)
    tb = SWA_TILE
    nk = n_ctx + 3 * tb
    kall = jnp.concatenate([kc_ref[...], kp_ref[...], ko_ref[...], kn_ref[...]], axis=0)
    vall = jnp.concatenate([vc_ref[...], vp_ref[...], vo_ref[...], vn_ref[...]], axis=0)
    col = lax.broadcasted_iota(jnp.int32, (tb, nk), 1)
    r = lax.broadcasted_iota(jnp.int32, (tb, nk), 0)
    j = col - n_ctx
    kpos = (i - 1) * tb + j
    local_ok = (jnp.abs(j - tb - r) <= SWA_WINDOW) & (kpos >= 0) & (kpos < n_lat) & (i < n_lat_tiles)
    valid = (col < n_ctx) | local_ok
    lane_kv = lax.broadcasted_iota(jnp.int32, (nk, LANES), 1)
    k_roll = pltpu.roll(kall, SWA_HEAD_DIM, 1)
    v_roll = pltpu.roll(vall, SWA_HEAD_DIM, 1)
    gw = SWA_GROUP * SWA_HEAD_DIM
    lane_g = lax.broadcasted_iota(jnp.int32, (tb, gw), 1) // SWA_HEAD_DIM
    heads = [(kvh, g) for kvh in range(SWA_KV_HEADS) for g in range(SWA_GROUP)]
    lo = lane_kv < SWA_HEAD_DIM
    kts, vts, qgs = [], [], []
    for kvh in range(SWA_KV_HEADS):
        k2 = jnp.where(lo, kall, k_roll) if kvh == 0 else jnp.where(lo, k_roll, kall)
        v2 = jnp.where(lo, vall, v_roll) if kvh == 0 else jnp.where(lo, v_roll, vall)
        kts.append(jnp.concatenate([k2, k2], axis=1))
        vts.append(jnp.concatenate([v2, v2], axis=1))
        qgs.append(q_ref[:, kvh * gw:(kvh + 1) * gw])
    scores = []
    for kvh, g in heads:
        qm = jnp.where(lane_g == g, qgs[kvh], jnp.zeros_like(qgs[kvh]))
        s = lax.dot_general(qm, kts[kvh], (((1,), (1,)), ((), ())), preferred_element_type=F32)
        scores.append(jnp.where(valid, s, -jnp.inf))
    probs, inv_denoms = [], []
    for (kvh, g), s in zip(heads, scores):
        sink = sink_ref[kvh * SWA_GROUP + g]
        m = jnp.maximum(jnp.max(s, axis=-1, keepdims=True), sink)
        p = jnp.exp(s - m)
        denom = jnp.sum(p, axis=-1, keepdims=True) + jnp.exp(sink - m)
        probs.append(p.astype(BF16))
        inv_denoms.append(1.0 / denom)
    for kvh in range(SWA_KV_HEADS):
        acc = jnp.zeros((tb, gw), F32)
        for g in range(SWA_GROUP):
            h = kvh * SWA_GROUP + g
            o = jnp.dot(probs[h], vts[kvh], preferred_element_type=F32) * inv_denoms[h]
            acc = jnp.where(lane_g == g, o, acc)
        o_ref[:, kvh * gw:(kvh + 1) * gw] = acc


def _swa(sink, sq, sk, sv, n_lat, n_ctx):
    t = n_lat + n_ctx
    tb = SWA_TILE
    nt = t // tb
    nlt = n_lat // tb
    kvw = SWA_KV_WIDTH
    prev_spec = pl.BlockSpec((tb, kvw), lambda i: (jnp.maximum(i - 1, 0), 0))
    own_spec = pl.BlockSpec((tb, kvw), lambda i: (i, 0))
    next_spec = pl.BlockSpec((tb, kvw), lambda i: (jnp.minimum(i + 1, nt - 1), 0))
    ctx_spec = pl.BlockSpec((n_ctx, kvw), lambda i: (n_lat // n_ctx, 0))
    return pl.pallas_call(
        functools.partial(_swa_kernel, n_lat_tiles=nlt, n_lat=n_lat, n_ctx=n_ctx),
        grid=(nt,),
        in_specs=[pl.BlockSpec(memory_space=pltpu.SMEM),
                  pl.BlockSpec((tb, SWA_WIDTH), lambda i: (i, 0)),
                  prev_spec, own_spec, next_spec, ctx_spec,
                  prev_spec, own_spec, next_spec, ctx_spec],
        out_specs=pl.BlockSpec((tb, SWA_WIDTH), lambda i: (i, 0)),
        out_shape=jax.ShapeDtypeStruct((t, SWA_WIDTH), F32),
        compiler_params=_cparams(("arbitrary",)),
        name="swa_attention",
    )(sink, sq, sk, sk, sk, sk, sv, sv, sv, sv)


def _mix_out_kernel(x_ref, mla_ref, swa_ref, cb_ref, cv_ref, hp_ref, hn_ref, cw_ref, go_ref, wo_ref,
                    mod_ref, gf_ref, wr_ref, br_ref,
                    xo_ref, h2_ref, te_ref, gt_ref, rk_ref, cnt_ref, cnt_sc,
                    *, n_lat_tiles, n_tiles):
    i = pl.program_id(0)
    tm = x_ref.shape[0]

    @pl.when(i == 0)
    def _():
        cnt_sc[...] = jnp.zeros(cnt_sc.shape, F32)

    seg_start = (i == 0) | (i == n_lat_tiles)
    seg_end = (i == n_lat_tiles - 1) | (i == n_tiles - 1)
    v = cv_ref[...]
    rowi = lax.broadcasted_iota(jnp.int32, v.shape, 0)
    left = jnp.where(seg_start, 0.0, hp_ref[SUBLANES - 1:SUBLANES, :])
    right = jnp.where(seg_end, 0.0, hn_ref[0:1, :])
    v_dn = jnp.where(rowi == 0, left, pltpu.roll(v, 1, 0))
    v_up = jnp.where(rowi == tm - 1, right, pltpu.roll(v, tm - 1, 0))
    conv = cb_ref[...] * (v_dn * cw_ref[0:1, :] + v * cw_ref[1:2, :] + v_up * cw_ref[2:3, :])

    o1, o2 = MLA_WIDTH, MLA_WIDTH + SWA_WIDTH
    ya = (_rms(mla_ref[...]) * go_ref[:, 0:o1]).astype(BF16)
    yb = (_rms(swa_ref[...]) * go_ref[:, o1:o2]).astype(BF16)
    yc = (_rms(conv) * go_ref[:, o2:MIX_WIDTH]).astype(BF16)
    o = (jnp.dot(ya, wo_ref[0:o1, :], preferred_element_type=F32)
         + jnp.dot(yb, wo_ref[o1:o2, :], preferred_element_type=F32)
         + jnp.dot(yc, wo_ref[o2:MIX_WIDTH, :], preferred_element_type=F32))
    x = x_ref[...] + mod_ref[2:3, :] * o
    xo_ref[...] = x

    h2 = _rms(x) * gf_ref[...]
    h2 = h2 * (1.0 + mod_ref[4:5, :]) + mod_ref[3:4, :]
    _store_slabs(h2_ref, h2)
    h_hi = h2.astype(BF16)
    h_lo = (h2 - h_hi.astype(F32)).astype(BF16)
    parts = (jnp.dot(h_hi, wr_ref[...], preferred_element_type=F32)
             + jnp.dot(h_lo, wr_ref[...], preferred_element_type=F32))
    logits = parts[:, 0:N_EXPERTS] + parts[:, N_EXPERTS:2 * N_EXPERTS] + br_ref[...]
    e_iota = lax.broadcasted_iota(jnp.int32, logits.shape, 1)
    lane = lax.broadcasted_iota(jnp.int32, (tm, LANES), 1)
    work = logits
    tops, idxs = [], []
    for _ in range(TOP_K):
        m = jnp.max(work, axis=-1, keepdims=True)
        idx = jnp.min(jnp.where(work == m, e_iota, N_EXPERTS), axis=-1, keepdims=True)
        tops.append(m)
        idxs.append(idx)
        work = jnp.where(e_iota == idx, -jnp.inf, work)
    exps = [jnp.exp(tk_ - tops[0]) for tk_ in tops]
    denom = exps[0] + exps[1] + exps[2] + exps[3]

    onehots = [(e_iota == idx).astype(F32) for idx in idxs]
    sel = onehots[0] + onehots[1] + onehots[2] + onehots[3]
    rr = lax.broadcasted_iota(jnp.int32, (tm, tm), 0)
    cc = lax.broadcasted_iota(jnp.int32, (tm, tm), 1)
    tri = (rr > cc).astype(BF16)
    before = jnp.dot(tri, sel.astype(BF16), preferred_element_type=F32) + cnt_sc[...]
    cnt_new = cnt_sc[...] + jnp.sum(sel, axis=0, keepdims=True)
    cnt_sc[...] = cnt_new
    cnt_ref[...] = jnp.broadcast_to(cnt_new, cnt_ref.shape).astype(jnp.int32)

    te = jnp.zeros((tm, LANES), jnp.int32)
    rk = jnp.zeros((tm, LANES), jnp.int32)
    for k in range(TOP_K):
        rank_k = jnp.sum(before * onehots[k], axis=-1, keepdims=True).astype(jnp.int32)
        te = jnp.where(lane == k, idxs[k], te)
        rk = jnp.where(lane == k, rank_k, rk)
        gt_ref[:, k * LANES:(k + 1) * LANES] = jnp.broadcast_to(exps[k] / denom, (tm, LANES))
    te_ref[...] = te
    rk_ref[...] = rk


def _mix_out(xs, mla, swa, cb, cv, conv_w, g_out, w_out_b, mod, g_ffn, w_r, b_r, n_lat_tiles):
    t, d = xs.shape
    tm = ROW_TILE
    nt = t // tm
    hb = tm // SUBLANES
    row = lambda w: pl.BlockSpec((tm, w), lambda i: (i, 0))
    full = lambda a: pl.BlockSpec(a.shape, lambda i: (0,) * a.ndim)
    return pl.pallas_call(
        functools.partial(_mix_out_kernel, n_lat_tiles=n_lat_tiles, n_tiles=nt),
        grid=(nt,),
        in_specs=[row(d), row(MLA_WIDTH), row(SWA_WIDTH), row(CONV_WIDTH), row(CONV_WIDTH),
                  pl.BlockSpec((SUBLANES, CONV_WIDTH), lambda i: (jnp.maximum(i * hb - 1, 0), 0)),
                  pl.BlockSpec((SUBLANES, CONV_WIDTH), lambda i: (jnp.minimum((i + 1) * hb, nt * hb - 1), 0)),
                  full(conv_w), full(g_out), full(w_out_b),
                  pl.BlockSpec((None, N_MOD, d), lambda i: (jnp.where(i < n_lat_tiles, 0, 1), 0, 0)),
                  full(g_ffn), full(w_r), full(b_r)],
        out_specs=[row(d), pl.BlockSpec((tm * _slab_rows(d), LANES), lambda i: (i, 0)),
                   row(LANES), row(TOP_K * LANES), row(LANES),
                   pl.BlockSpec((SUBLANES, N_EXPERTS), lambda i: (0, 0))],
        out_shape=[jax.ShapeDtypeStruct((t, d), F32),
                   jax.ShapeDtypeStruct((t * _slab_rows(d), LANES), U32),
                   jax.ShapeDtypeStruct((t, LANES), jnp.int32), jax.ShapeDtypeStruct((t, TOP_K * LANES), F32),
                   jax.ShapeDtypeStruct((t, LANES), jnp.int32),
                   jax.ShapeDtypeStruct((SUBLANES, N_EXPERTS), jnp.int32)],
        scratch_shapes=[pltpu.VMEM((1, N_EXPERTS), F32)],
        compiler_params=_cparams(("arbitrary",)),
        name="mix_out_ffn_route",
    )(xs, mla, swa, cb, cv, cv, cv, conv_w, g_out, w_out_b, mod, g_ffn, w_r, b_r)


def _dispatch_kernel(pstart_ref, pend_ref, te_ref, rk_ref, h_ref, xs_ref, zero_buf, sem, zsem, *, sr, nr):
    tm = h_ref.shape[0] // sr
    bm = zero_buf.shape[0] // sr

    @pl.when(pl.program_id(0) == 0)
    def _():
        zero_buf[...] = jnp.zeros(zero_buf.shape, U32)

        def tail_copy(e):
            last = pl.multiple_of(jnp.maximum(pend_ref[e] - bm, 0) * sr, bm * sr)
            return pltpu.make_async_copy(zero_buf, xs_ref.at[pl.ds(last, bm * sr), :], zsem)

        def zstart(e, c):
            @pl.when(pend_ref[e] > pstart_ref[e])
            def _():
                tail_copy(e).start()
            return c

        def zwait(e, c):
            @pl.when(pend_ref[e] > pstart_ref[e])
            def _():
                tail_copy(e).wait()
            return c

        lax.fori_loop(0, N_EXPERTS, zstart, 0)
        lax.fori_loop(0, N_EXPERTS, zwait, 0)

    def row_copy(r, k):
        a = r * TOP_K + k
        dest = pl.multiple_of((pstart_ref[te_ref[a]] + rk_ref[a]) * sr, sr)
        return pltpu.make_async_copy(h_ref.at[pl.ds(pl.multiple_of(r * sr, sr), nr), :],
                                     xs_ref.at[pl.ds(dest, nr), :], sem)

    def issue(r, c):
        for k in range(TOP_K):
            row_copy(r, k).start()
        return c

    def drain(r, c):
        for k in range(TOP_K):
            row_copy(r, k).wait()
        return c

    lax.fori_loop(0, tm, issue, 0)
    lax.fori_loop(0, tm, drain, 0)


def _dispatch(pstart, pend, te_flat, rk_flat, h2, n_slots, sr, nr):
    w = h2.shape[1]
    tm = ROW_TILE
    flat = pl.BlockSpec((tm * TOP_K,), lambda i, ps, pe: (i,), memory_space=pltpu.SMEM)
    grid_spec = pltpu.PrefetchScalarGridSpec(
        num_scalar_prefetch=2,
        grid=(h2.shape[0] // (tm * sr),),
        in_specs=[flat, flat, pl.BlockSpec((tm * sr, w), lambda i, ps, pe: (i, 0))],
        out_specs=pl.BlockSpec(memory_space=pl.ANY),
        scratch_shapes=[pltpu.VMEM((MOE_ROWS * sr, w), U32), pltpu.SemaphoreType.DMA(()),
                        pltpu.SemaphoreType.DMA(())],
    )
    return pl.pallas_call(
        functools.partial(_dispatch_kernel, sr=sr, nr=nr),
        grid_spec=grid_spec,
        out_shape=jax.ShapeDtypeStruct((n_slots * sr, w), U32),
        compiler_params=_cparams(("arbitrary",)),
        name="moe_dispatch",
    )(pstart, pend, te_flat, rk_flat, h2)


def _expert_kernel(be_ref, na_ref, x_ref, w1_ref, b1_ref, w2_ref, b2_ref, sel_ref, y_ref, w1b, w2b):
    b = pl.program_id(0)
    active = b < na_ref[0]
    new_expert = (b == 0) | (be_ref[b] != be_ref[jnp.maximum(b - 1, 0)])

    @pl.when(active & new_expert)
    def _():
        w1b[...] = w1_ref[...].astype(BF16)
        w2b[...] = w2_ref[...].astype(BF16)

    @pl.when(active)
    def _():
        sr = _slab_rows(w1b.shape[0])
        bm = x_ref.shape[0] // sr
        gu = b1_ref[...]
        for j in range(_slab_pieces(w1b.shape[0])):
            xj = _load_slab_piece(x_ref, j, bm, sr).astype(BF16)
            gu = gu + jnp.dot(xj, w1b[2 * j * LANES:(2 * j + 2) * LANES, :], preferred_element_type=F32)
        gate = jnp.minimum(gu, SWIGLU_LIMIT)
        a = gate * jax.nn.sigmoid(SWIGLU_ALPHA * gate)
        u = jnp.clip(gu, -SWIGLU_LIMIT, SWIGLU_LIMIT) + 1.0
        act = a * pltpu.roll(u, u.shape[1] - 1, 1)
        act = jnp.dot(act.astype(BF16), sel_ref[...], preferred_element_type=F32)
        y = jnp.dot(act.astype(BF16), w2b[...], preferred_element_type=F32) + b2_ref[...]
        _store_slabs(y_ref, y)


def _experts(layer, block_expert, n_active, xs, w1, b1, w2, b2, sel):
    d = w1.shape[2]
    sr = _slab_rows(d)
    w = xs.shape[1]
    bm = MOE_ROWS
    nb = xs.shape[0] // (bm * sr)
    blk = lambda b, be, na: (jnp.minimum(b, na[0] - 1), 0)
    wsel = lambda b, be, na: (layer, be[jnp.minimum(b, na[0] - 1)], 0, 0)
    grid_spec = pltpu.PrefetchScalarGridSpec(
        num_scalar_prefetch=2,
        grid=(nb,),
        in_specs=[pl.BlockSpec((bm * sr, w), blk),
                  pl.BlockSpec((None, None, d, 2 * D_EXPERT), wsel),
                  pl.BlockSpec((None, None, 1, 2 * D_EXPERT), wsel),
                  pl.BlockSpec((None, None, D_EXPERT, d), wsel),
                  pl.BlockSpec((None, None, 1, d), wsel),
                  pl.BlockSpec((2 * D_EXPERT, D_EXPERT), lambda b, be, na: (0, 0))],
        out_specs=pl.BlockSpec((bm * sr, w), blk),
        scratch_shapes=[pltpu.VMEM((d, 2 * D_EXPERT), BF16), pltpu.VMEM((D_EXPERT, d), BF16)],
    )
    return pl.pallas_call(
        _expert_kernel,
        grid_spec=grid_spec,
        out_shape=jax.ShapeDtypeStruct(xs.shape, U32),
        compiler_params=_cparams(("arbitrary",)),
        name="moe_experts",
    )(block_expert, n_active, xs, w1, b1, w2, b2, sel)


def _combine_kernel(pstart_ref, te_ref, rk_ref, x_ref, gt_ref, mod_ref, y_ref, *rest, final):
    if final:
        gfin_ref, o_ref, buf, sem = rest
    else:
        o_ref, buf, sem = rest
    tm, d = x_ref.shape
    sr = _slab_rows(d)
    nr = _slab_pieces(d)

    def row_copy(r, k):
        a = r * TOP_K + k
        src = pl.multiple_of((pstart_ref[te_ref[a]] + rk_ref[a]) * sr, sr)
        return pltpu.make_async_copy(y_ref.at[pl.ds(src, nr), :],
                                     buf.at[k, pl.ds(pl.multiple_of(r * sr, sr), nr), :], sem)

    def issue(r, c):
        for k in range(TOP_K):
            row_copy(r, k).start()
        return c

    def drain(r, c):
        for k in range(TOP_K):
            row_copy(r, k).wait()
        return c

    lax.fori_loop(0, tm, issue, 0)
    lax.fori_loop(0, tm, drain, 0)
    for j in range(nr):
        f = None
        for k in range(TOP_K):
            gk = gt_ref[:, k * LANES:(k + 1) * LANES]
            term = jnp.concatenate([gk, gk], axis=1) * _load_slab_piece(buf.at[k], j, tm, sr)
            f = term if f is None else f + term
        cols = slice(2 * j * LANES, (2 * j + 2) * LANES)
        o_ref[:, cols] = x_ref[:, cols] + mod_ref[5:6, cols] * f
    if final:
        o_ref[...] = _rms(o_ref[...]) * gfin_ref[...]


def _combine(pstart, te_flat, rk_flat, xs, gates, mod, y, n_lat_tiles, g_final=None):
    t, d = xs.shape
    tm = ROW_TILE
    n_tiles = t // tm if g_final is None else n_lat_tiles
    flat = pl.BlockSpec((tm * TOP_K,), lambda i, ps: (i,), memory_space=pltpu.SMEM)
    in_specs = [flat, flat,
                pl.BlockSpec((tm, d), lambda i, ps: (i, 0)),
                pl.BlockSpec((tm, TOP_K * LANES), lambda i, ps: (i, 0)),
                pl.BlockSpec((None, N_MOD, d), lambda i, ps: (jnp.where(i < n_lat_tiles, 0, 1), 0, 0)),
                pl.BlockSpec(memory_space=pl.ANY)]
    args = [pstart, te_flat, rk_flat, xs, gates, mod, y]
    if g_final is not None:
        in_specs.append(pl.BlockSpec((1, d), lambda i, ps: (0, 0)))
        args.append(g_final)
    grid_spec = pltpu.PrefetchScalarGridSpec(
        num_scalar_prefetch=1,
        grid=(n_tiles,),
        in_specs=in_specs,
        out_specs=pl.BlockSpec((tm, d), lambda i, ps: (i, 0)),
        scratch_shapes=[pltpu.VMEM((TOP_K, tm * _slab_rows(d), LANES), U32), pltpu.SemaphoreType.DMA(())],
    )
    return pl.pallas_call(
        functools.partial(_combine_kernel, final=g_final is not None),
        grid_spec=grid_spec,
        out_shape=jax.ShapeDtypeStruct((n_tiles * tm, d), F32),
        compiler_params=_cparams(("arbitrary",)),
        name="moe_combine" if g_final is None else "moe_combine_final_norm",
    )(*args)


def _rope_tables(n_lat, n_ctx):
    rows = n_lat // GRID_W
    row_id, col_id = jnp.meshgrid(jnp.arange(rows), jnp.arange(GRID_W), indexing="ij")
    half = MLA_ROPE // 2
    inv_freq = ROPE_THETA ** (-jnp.arange(0, half, 2, dtype=F32) / half)

    def axis_angles(pos):
        a = pos.reshape(-1).astype(F32)[:, None] * inv_freq[None, :]
        return jnp.concatenate([a, a], axis=-1)

    ang = jnp.concatenate([axis_angles(row_id), axis_angles(col_id)], axis=-1)
    cos, sin = jnp.cos(ang), jnp.sin(ang)
    sign = jnp.where((jnp.arange(MLA_ROPE) % 32) < 16, -1.0, 1.0).astype(F32)
    sin = sin * sign[None, :]
    cos = jnp.concatenate([cos, jnp.ones((n_ctx, MLA_ROPE), F32)], axis=0)
    sin = jnp.concatenate([sin, jnp.zeros((n_ctx, MLA_ROPE), F32)], axis=0)
    return jnp.concatenate([cos, cos], axis=1), jnp.concatenate([sin, sin], axis=1)


def _split_in_proj(w):
    d = w.shape[0]
    sizes = (MLA_Q_RANK, MLA_KV_RANK, MLA_ROPE, SWA_WIDTH, SWA_KV_WIDTH, SWA_KV_WIDTH,
             CONV_WIDTH, CONV_WIDTH, CONV_WIDTH)
    offs = np.concatenate([[0], np.cumsum(sizes)])
    part = [w[:, offs[j]:offs[j + 1]] for j in range(len(sizes))]
    cq, ckv, kr, sq, sk, sv, cb, cc, cx = part
    return jnp.concatenate([cq, ckv, sq, sk, sv, cb, cc, cx, kr, jnp.zeros((d, LANES - MLA_ROPE), w.dtype)],
                           axis=1).astype(BF16)


def _split_uq(w):
    r = w.shape[0]
    w = w.reshape(r, MLA_HEADS, MLA_NOPE + MLA_ROPE)
    wn = w[:, :, :MLA_NOPE].reshape(r, MLA_HEADS * MLA_NOPE)
    wr = jnp.concatenate([w[:, :, MLA_NOPE:], jnp.zeros((r, MLA_HEADS, LANES - MLA_ROPE), w.dtype)], axis=2)
    return wn.astype(BF16), wr.reshape(r, MLA_HEADS * LANES).astype(BF16)


def _split_hi_lo(w):
    hi = w.astype(BF16)
    lo = (w - hi.astype(F32)).astype(BF16)
    return jnp.concatenate([hi, lo], axis=1)


def kernel(x, c, ctx, c_ctx, w_ada, b_ada, g_mix, w_in, g_mla_q, g_mla_kv, w_mla_uq, w_mla_ukv,
           swa_sink, conv_w, g_out, w_out, g_ffn, w_router, b_router, w_exp1, b_exp1, w_exp2,
           b_exp2, g_final):
    bsz, n_lat, d = x.shape
    n_ctx = ctx.shape[1]
    depth = w_ada.shape[0]
    assert bsz == 1 and n_lat % ATTN_K_CHUNK == 0 and n_ctx == ROW_TILE and d % LANES == 0
    t = n_lat + n_ctx
    t_pad = -(-t // ATTN_K_CHUNK) * ATTN_K_CHUNK
    n_lat_tiles = n_lat // ROW_TILE

    xs = jnp.concatenate([x[0], ctx[0]], axis=0)
    cos, sin = _rope_tables(n_lat, n_ctx)
    mods = _ada(c, c_ctx, w_ada, b_ada).reshape(depth, 2, N_MOD, d)

    n_blocks = -(-t * TOP_K // MOE_ROWS) + N_EXPERTS
    n_slots = n_blocks * MOE_ROWS
    even_sel = (jnp.arange(2 * D_EXPERT)[:, None] == 2 * jnp.arange(D_EXPERT)[None, :]).astype(BF16)

    for l in range(depth):
        mod = mods[l]
        cq, ckv, kr, sq, sk, sv, cb, cv = _inproj(xs, g_mix[l][None], mod, cos, sin,
                                                  _split_in_proj(w_in[l]), n_lat_tiles)
        wqn, wqr = _split_uq(w_mla_uq[l])
        q, k, v = _mla_up(cq, ckv, kr, g_mla_q[l][None], g_mla_kv[l][None], cos, sin,
                          wqn, wqr, w_mla_ukv[l].astype(BF16), t_pad)
        mla = _mla_attention(q, k, v, n_lat, n_ctx)
        swa = _swa(swa_sink[l], sq, sk, sv, n_lat, n_ctx)
        xs, h2, top_e, gates, rank, counts = _mix_out(
            xs, mla, swa, cb, cv, conv_w[l], g_out[l][None], w_out[l].astype(BF16), mod,
            g_ffn[l][None], _split_hi_lo(w_router[l]), b_router[l][None], n_lat_tiles)

        cnt = counts[0]
        padded = (cnt + MOE_ROWS - 1) // MOE_ROWS * MOE_ROWS
        pend = jnp.cumsum(padded).astype(jnp.int32)
        pstart = (pend - padded).astype(jnp.int32)
        bstart = jnp.arange(n_blocks, dtype=jnp.int32) * MOE_ROWS
        block_expert = jnp.minimum(jnp.sum((pend[None, :] <= bstart[:, None]).astype(jnp.int32), axis=1),
                                   N_EXPERTS - 1).astype(jnp.int32)
        n_active = (pend[-1:] // MOE_ROWS).astype(jnp.int32)
        te_flat = top_e[:, :TOP_K].reshape(-1)
        rk_flat = rank[:, :TOP_K].reshape(-1)

        xg = _dispatch(pstart, pend, te_flat, rk_flat, h2, n_slots, _slab_rows(d), _slab_pieces(d))
        y = _experts(l, block_expert, n_active, xg, w_exp1, b_exp1[:, :, None, :],
                     w_exp2, b_exp2[:, :, None, :], even_sel)
        xs = _combine(pstart, te_flat, rk_flat, xs, gates, mod, y, n_lat_tiles,
                      g_final=g_final[None] if l == depth - 1 else None)

    return xs.reshape(1, n_lat, d)
```

```python
import functools

import numpy as np
import jax
import jax.numpy as jnp
from jax import lax
from jax.experimental import pallas as pl
from jax.experimental.pallas import tpu as pltpu

F32 = jnp.float32
BF16 = jnp.bfloat16

GRID_W = 64
ROPE_THETA = 10000.0
NORM_EPS = 1e-6
N_MOD = 6
MLA_HEADS = 8
MLA_Q_RANK = 512
MLA_KV_RANK = 256
MLA_NOPE = 128
MLA_ROPE = 64
MLA_V = 128
MLA_WIDTH = MLA_HEADS * MLA_V
MLA_SCALE = (MLA_NOPE + MLA_ROPE) ** -0.5
SWA_HEADS = 8
SWA_KV_HEADS = 2
SWA_GROUP = SWA_HEADS // SWA_KV_HEADS
SWA_HEAD_DIM = 64
SWA_WINDOW = 128
SWA_WIDTH = SWA_HEADS * SWA_HEAD_DIM
SWA_KV_WIDTH = SWA_KV_HEADS * SWA_HEAD_DIM
SWA_SCALE = SWA_HEAD_DIM ** -0.5
CONV_WIDTH = 512
CONV_K = 3
MIX_WIDTH = MLA_WIDTH + SWA_WIDTH + CONV_WIDTH
N_EXPERTS = 32
TOP_K = 4
D_EXPERT = 512
SWIGLU_LIMIT = 7.0
SWIGLU_ALPHA = 1.702
LOG2E = 1.4426950408889634

LANES = 128
SUBLANES = 8
VMEM_LIMIT = 56 * 1024 * 1024

ROW_TILE = 256
SWA_TILE = 128
MOE_ROWS = 256
ATTN_Q_TILE = 1024
ATTN_K_CHUNK = 512
ATTN_UNROLL = 8
PAD_KEY_SCORE = -1e30

_O_CQ = 0
_O_CKV = _O_CQ + MLA_Q_RANK
_O_SQ = _O_CKV + MLA_KV_RANK
_O_SK = _O_SQ + SWA_WIDTH
_O_SV = _O_SK + SWA_KV_WIDTH
_O_CB = _O_SV + SWA_KV_WIDTH
_O_CC = _O_CB + CONV_WIDTH
_O_CX = _O_CC + CONV_WIDTH
_O_KR = _O_CX + CONV_WIDTH
N_IN_PAD = _O_KR + LANES


def _cparams(sem):
    return pltpu.CompilerParams(dimension_semantics=sem, vmem_limit_bytes=VMEM_LIMIT)


def _rms(x):
    return x * lax.rsqrt(jnp.mean(x * x, axis=-1, keepdims=True) + NORM_EPS)


U32 = jnp.uint32


def _slab_pieces(d):
    return d // (2 * LANES)


def _slab_rows(d):
    n = _slab_pieces(d)
    return n + 4 if n % SUBLANES == 0 else n


def _pack_pair(a, b):
    ah = lax.bitcast_convert_type(a.astype(BF16).astype(F32), U32)
    bh = lax.bitcast_convert_type(b.astype(BF16).astype(F32), U32)
    return ah | (bh >> 16)


def _unpack_pair(u):
    a = lax.bitcast_convert_type(u & U32(0xFFFF0000), F32)
    b = lax.bitcast_convert_type(u << 16, F32)
    return a, b


def _store_slabs(ref, val):
    rows, d = val.shape
    sr = _slab_rows(d)
    for j in range(_slab_pieces(d)):
        lo = 2 * j * LANES
        ref[pl.ds(j, rows, stride=sr), :] = _pack_pair(val[:, lo:lo + LANES], val[:, lo + LANES:lo + 2 * LANES])


def _load_slab_piece(ref, j, rows, sr):
    a, b = _unpack_pair(ref[pl.ds(j, rows, stride=sr), :])
    return jnp.concatenate([a, b], axis=1)


def _rope(u, cos, sin_signed):
    w = u.shape[-1]
    reps = w // LANES
    if reps > 1:
        cos = jnp.concatenate([cos] * reps, axis=1)
        sin_signed = jnp.concatenate([sin_signed] * reps, axis=1)
    lane = lax.broadcasted_iota(jnp.int32, u.shape, 1)
    first = (lane % 32) < 16
    rot = jnp.where(first, pltpu.roll(u, w - 16, 1), pltpu.roll(u, 16, 1))
    return u * cos + rot * sin_signed


def _ada_kernel(s_ref, w_ref, b_ref, o_ref, *, chunk):
    d, tn = w_ref.shape

    def body(i, accs):
        a0, a1 = accs
        r0 = pl.multiple_of(i * chunk, chunk)
        w = w_ref[pl.ds(r0, chunk), :]
        s = s_ref[pl.ds(r0, chunk), :]
        s = s * jax.nn.sigmoid(s)
        p0 = (w * s[:, 0:1]).reshape(chunk // SUBLANES, SUBLANES, tn).sum(axis=0)
        p1 = (w * s[:, 1:2]).reshape(chunk // SUBLANES, SUBLANES, tn).sum(axis=0)
        return a0 + p0, a1 + p1

    z = jnp.zeros((SUBLANES, tn), F32)
    a0, a1 = lax.fori_loop(0, d // chunk, body, (z, z))
    b = b_ref[...]
    o_ref[0:1, :] = jnp.sum(a0, axis=0, keepdims=True) + b
    o_ref[1:2, :] = jnp.sum(a1, axis=0, keepdims=True) + b


def _ada(c, c_ctx, w_ada, b_ada):
    depth, d, n = w_ada.shape
    tn = 1024 if n % 1024 == 0 else 512
    chunk = 64
    s = jnp.stack([c.reshape(d), c_ctx.reshape(d)], axis=1)
    return pl.pallas_call(
        functools.partial(_ada_kernel, chunk=chunk),
        grid=(depth, n // tn),
        in_specs=[
            pl.BlockSpec((d, 2), lambda l, j: (0, 0)),
            pl.BlockSpec((None, d, tn), lambda l, j: (l, 0, j)),
            pl.BlockSpec((None, 1, tn), lambda l, j: (l, 0, j)),
        ],
        out_specs=pl.BlockSpec((None, 2, tn), lambda l, j: (l, 0, j)),
        out_shape=jax.ShapeDtypeStruct((depth, 2, n), F32),
        compiler_params=_cparams(("arbitrary", "arbitrary")),
        name="ada_mod",
    )(s, w_ada, b_ada.reshape(depth, 1, n))


def _inproj_kernel(x_ref, g_ref, mod_ref, cos_ref, sin_ref, w_ref,
                   cq_ref, ckv_ref, kr_ref, sq_ref, sk_ref, sv_ref, cb_ref, cv_ref):
    x = x_ref[...]
    h = _rms(x) * g_ref[...]
    h = h * (1.0 + mod_ref[1:2, :]) + mod_ref[0:1, :]
    hb = h.astype(BF16)
    cos = cos_ref[...]
    sin = sin_ref[...]

    def proj(a, width):
        return jnp.dot(hb, w_ref[:, a:a + width], preferred_element_type=F32)

    cq_ref[...] = proj(_O_CQ, MLA_Q_RANK)
    ckv_ref[...] = proj(_O_CKV, MLA_KV_RANK)
    kr_ref[...] = _rope(proj(_O_KR, LANES), cos, sin).astype(BF16)
    sq_ref[...] = (_rope(proj(_O_SQ, SWA_WIDTH), cos, sin) * SWA_SCALE).astype(BF16)
    sk_ref[...] = _rope(proj(_O_SK, SWA_KV_WIDTH), cos, sin).astype(BF16)
    sv_ref[...] = proj(_O_SV, SWA_KV_WIDTH).astype(BF16)
    cb_ref[...] = proj(_O_CB, CONV_WIDTH)
    cv_ref[...] = proj(_O_CC, CONV_WIDTH) * proj(_O_CX, CONV_WIDTH)


def _inproj(xs, g, mod, cos, sin, w_in_p, n_lat_tiles):
    t, d = xs.shape
    tm = ROW_TILE
    row = lambda w: pl.BlockSpec((tm, w), lambda i: (i, 0))
    outs = [
        (MLA_Q_RANK, F32), (MLA_KV_RANK, F32), (LANES, BF16), (SWA_WIDTH, BF16),
        (SWA_KV_WIDTH, BF16), (SWA_KV_WIDTH, BF16), (CONV_WIDTH, F32), (CONV_WIDTH, F32),
    ]
    return pl.pallas_call(
        _inproj_kernel,
        grid=(t // tm,),
        in_specs=[
            row(d),
            pl.BlockSpec((1, d), lambda i: (0, 0)),
            pl.BlockSpec((None, N_MOD, d), lambda i: (jnp.where(i < n_lat_tiles, 0, 1), 0, 0)),
            row(LANES), row(LANES),
            pl.BlockSpec((d, N_IN_PAD), lambda i: (0, 0)),
        ],
        out_specs=[row(w) for w, _ in outs],
        out_shape=[jax.ShapeDtypeStruct((t, w), dt) for w, dt in outs],
        compiler_params=_cparams(("arbitrary",)),
        name="mixer_in_proj",
    )(xs, g, mod, cos, sin, w_in_p)


def _mla_up_kernel(cq_ref, ckv_ref, kr_ref, gq_ref, gkv_ref, cos_ref, sin_ref,
                   wqn_ref, wqr_ref, wkv_ref, q_ref, k_ref, v_ref, *, n_real_tiles):
    i = pl.program_id(0)
    tm = cq_ref.shape[0]
    last_lane = lax.broadcasted_iota(jnp.int32, (tm, LANES), 1) == LANES - 1

    @pl.when(i < n_real_tiles)
    def _():
        hq = (_rms(cq_ref[...]) * gq_ref[...]).astype(BF16)
        qs = MLA_SCALE * LOG2E
        qn = jnp.dot(hq, wqn_ref[...], preferred_element_type=F32) * qs
        qr = _rope(jnp.dot(hq, wqr_ref[...], preferred_element_type=F32), cos_ref[...], sin_ref[...]) * qs
        hk = (_rms(ckv_ref[...]) * gkv_ref[...]).astype(BF16)
        kv = jnp.dot(hk, wkv_ref[...], preferred_element_type=F32)
        kr = kr_ref[...]
        ones = jnp.ones((tm, LANES), BF16)
        for h in range(MLA_HEADS):
            q_ref[h, :, 0:LANES] = qn[:, h * LANES:(h + 1) * LANES].astype(BF16)
            q_ref[h, :, LANES:2 * LANES] = jnp.where(last_lane, 1.0, qr[:, h * LANES:(h + 1) * LANES]).astype(BF16)
            k_ref[h, :, 0:LANES] = kv[:, h * 2 * LANES:h * 2 * LANES + LANES].astype(BF16)
            k_ref[h, :, LANES:2 * LANES] = kr
            v_ref[h, :, 0:LANES] = kv[:, h * 2 * LANES + LANES:(h + 1) * 2 * LANES].astype(BF16)
            v_ref[h, :, LANES:2 * LANES] = ones

    @pl.when(i >= n_real_tiles)
    def _():
        q_ref[...] = jnp.zeros(q_ref.shape, BF16)
        v_ref[...] = jnp.zeros(v_ref.shape, BF16)
        zero = jnp.zeros((tm, LANES), BF16)
        flag = jnp.where(last_lane, PAD_KEY_SCORE, 0.0).astype(BF16)
        for h in range(MLA_HEADS):
            k_ref[h, :, 0:LANES] = zero
            k_ref[h, :, LANES:2 * LANES] = flag


def _mla_up(cq, ckv, kr, gq, gkv, cos, sin, wqn, wqr, wkv, t_pad):
    t = cq.shape[0]
    tm = ROW_TILE
    nr = t // tm
    row = lambda w: pl.BlockSpec((tm, w), lambda i: (jnp.minimum(i, nr - 1), 0))
    full = lambda a: pl.BlockSpec(a.shape, lambda i: (0,) * a.ndim)
    hd = pl.BlockSpec((MLA_HEADS, tm, 2 * LANES), lambda i: (0, i, 0))
    shp = jax.ShapeDtypeStruct((MLA_HEADS, t_pad, 2 * LANES), BF16)
    return pl.pallas_call(
        functools.partial(_mla_up_kernel, n_real_tiles=nr),
        grid=(t_pad // tm,),
        in_specs=[row(MLA_Q_RANK), row(MLA_KV_RANK), row(LANES), full(gq), full(gkv),
                  row(LANES), row(LANES), full(wqn), full(wqr), full(wkv)],
        out_specs=[hd, hd, hd],
        out_shape=[shp, shp, shp],
        compiler_params=_cparams(("arbitrary",)),
        name="mla_up_proj",
    )(cq, ckv, kr, gq, gkv, cos, sin, wqn, wqr, wkv)


def _mla_attn_kernel(q_ref, k_ref, v_ref, o_ref, s_buf, m_sc, acc_sc, *, n_chunks, ch, unroll):
    reps = ch // LANES
    m_sc[...] = jnp.full(m_sc.shape, -jnp.inf, F32)
    acc_sc[...] = jnp.zeros(acc_sc.shape, F32)

    def scores(c, slot):
        kc = k_ref[pl.ds(pl.multiple_of(c * ch, ch), ch), :]
        s_buf[slot] = lax.dot_general(q_ref[...], kc, (((1,), (1,)), ((), ())), preferred_element_type=F32)

    def softmax_pv(c, slot):
        s = s_buf[slot]
        m_prev = m_sc[...]
        m_new = jnp.maximum(m_prev, jnp.max(s, axis=-1, keepdims=True))
        alpha = jnp.exp2(m_prev - m_new)
        p = jnp.exp2(s - jnp.concatenate([m_new] * reps, axis=1)).astype(BF16)
        vc = v_ref[pl.ds(pl.multiple_of(c * ch, ch), ch), :]
        pv = jnp.dot(p, vc, preferred_element_type=F32)
        acc_sc[...] = acc_sc[...] * jnp.concatenate([alpha, alpha], axis=1) + pv
        m_sc[...] = m_new

    scores(0, 0)

    def trip(j, carry):
        for u in range(unroll):
            scores(j * unroll + u + 1, (u + 1) % 2)
            softmax_pv(j * unroll + u, u % 2)
        return carry

    n_trips = (n_chunks - 1) // unroll
    lax.fori_loop(0, n_trips, trip, 0)
    for c in range(n_trips * unroll, n_chunks):
        if c + 1 < n_chunks:
            scores(c + 1, (c + 1) % 2)
        softmax_pv(c, c % 2)
    acc = acc_sc[...]
    o_ref[...] = acc[:, 0:MLA_V] / acc[:, MLA_V:2 * MLA_V]


def _mla_ctx_kernel(q_ref, k_ref, v_ref, prev_ref, o_ref):
    del prev_ref
    s = lax.dot_general(q_ref[...], k_ref[...], (((1,), (1,)), ((), ())), preferred_element_type=F32)
    p = jnp.exp2(s - jnp.max(s, axis=-1, keepdims=True))
    pv = jnp.dot(p.astype(BF16), v_ref[...], preferred_element_type=F32)
    o_ref[...] = pv[:, 0:MLA_V] / pv[:, MLA_V:2 * MLA_V]


def _mla_attention(q, k, v, n_lat, n_ctx):
    t = n_lat + n_ctx
    t_pad = k.shape[1]
    tq = ATTN_Q_TILE if n_lat % ATTN_Q_TILE == 0 else ROW_TILE
    ch = ATTN_K_CHUNK
    wide = 2 * LANES
    out = pl.pallas_call(
        functools.partial(_mla_attn_kernel, n_chunks=t_pad // ch, ch=ch, unroll=ATTN_UNROLL),
        grid=(MLA_HEADS, n_lat // tq),
        in_specs=[pl.BlockSpec((None, tq, wide), lambda h, i: (h, i, 0)),
                  pl.BlockSpec((None, t_pad, wide), lambda h, i: (h, 0, 0)),
                  pl.BlockSpec((None, t_pad, wide), lambda h, i: (h, 0, 0))],
        out_specs=pl.BlockSpec((tq, MLA_V), lambda h, i: (i, h)),
        out_shape=jax.ShapeDtypeStruct((t, MLA_WIDTH), F32),
        scratch_shapes=[pltpu.VMEM((2, tq, ch), F32), pltpu.VMEM((tq, LANES), F32), pltpu.VMEM((tq, wide), F32)],
        compiler_params=_cparams(("arbitrary", "arbitrary")),
        name="mla_attention",
    )(q, k, v)
    cb = n_lat // n_ctx
    return pl.pallas_call(
        _mla_ctx_kernel,
        grid=(MLA_HEADS,),
        in_specs=[pl.BlockSpec((None, n_ctx, wide), lambda h: (h, cb, 0)),
                  pl.BlockSpec((None, n_ctx, wide), lambda h: (h, cb, 0)),
                  pl.BlockSpec((None, n_ctx, wide), lambda h: (h, cb, 0)),
                  pl.BlockSpec(memory_space=pl.ANY)],
        out_specs=pl.BlockSpec((n_ctx, MLA_V), lambda h: (cb, h)),
        out_shape=jax.ShapeDtypeStruct((t, MLA_WIDTH), F32),
        input_output_aliases={3: 0},
        compiler_params=_cparams(("arbitrary",)),
        name="mla_attention_ctx",
    )(q, k, v, out)


def _swa_kernel(sink_ref, q_ref, kp_ref, ko_ref, kn_ref, kc_ref, vp_ref, vo_ref, vn_ref, vc_ref, o_ref,
                *, n_lat_tiles, n_lat, n_ctx):
    i = pl.program_id(0)
    tb = SWA_TILE
    nk = n_ctx + 3 * tb
    kall = jnp.concatenate([kc_ref[...], kp_ref[...], ko_ref[...], kn_ref[...]], axis=0)
    vall = jnp.concatenate([vc_ref[...], vp_ref[...], vo_ref[...], vn_ref[...]], axis=0)
    col = lax.broadcasted_iota(jnp.int32, (tb, nk), 1)
    r = lax.broadcasted_iota(jnp.int32, (tb, nk), 0)
    j = col - n_ctx
    kpos = (i - 1) * tb + j
    local_ok = (jnp.abs(j - tb - r) <= SWA_WINDOW) & (kpos >= 0) & (kpos < n_lat) & (i < n_lat_tiles)
    valid = (col < n_ctx) | local_ok
    lane_kv = lax.broadcasted_iota(jnp.int32, (nk, LANES), 1)
    k_roll = pltpu.roll(kall, SWA_HEAD_DIM, 1)
    v_roll = pltpu.roll(vall, SWA_HEAD_DIM, 1)
    gw = SWA_GROUP * SWA_HEAD_DIM
    lane_g = lax.broadcasted_iota(jnp.int32, (tb, gw), 1) // SWA_HEAD_DIM
    heads = [(kvh, g) for kvh in range(SWA_KV_HEADS) for g in range(SWA_GROUP)]
    lo = lane_kv < SWA_HEAD_DIM
    kts, vts, qgs = [], [], []
    for kvh in range(SWA_KV_HEADS):
        k2 = jnp.where(lo, kall, k_roll) if kvh == 0 else jnp.where(lo, k_roll, kall)
        v2 = jnp.where(lo, vall, v_roll) if kvh == 0 else jnp.where(lo, v_roll, vall)
        kts.append(jnp.concatenate([k2, k2], axis=1))
        vts.append(jnp.concatenate([v2, v2], axis=1))
        qgs.append(q_ref[:, kvh * gw:(kvh + 1) * gw])
    scores = []
    for kvh, g in heads:
        qm = jnp.where(lane_g == g, qgs[kvh], jnp.zeros_like(qgs[kvh]))
        s = lax.dot_general(qm, kts[kvh], (((1,), (1,)), ((), ())), preferred_element_type=F32)
        scores.append(jnp.where(valid, s, -jnp.inf))
    probs, inv_denoms = [], []
    for (kvh, g), s in zip(heads, scores):
        sink = sink_ref[kvh * SWA_GROUP + g]
        m = jnp.maximum(jnp.max(s, axis=-1, keepdims=True), sink)
        p = jnp.exp(s - m)
        denom = jnp.sum(p, axis=-1, keepdims=True) + jnp.exp(sink - m)
        probs.append(p.astype(BF16))
        inv_denoms.append(1.0 / denom)
    for kvh in range(SWA_KV_HEADS):
        acc = jnp.zeros((tb, gw), F32)
        for g in range(SWA_GROUP):
            h = kvh * SWA_GROUP + g
            o = jnp.dot(probs[h], vts[kvh], preferred_element_type=F32) * inv_denoms[h]
            acc = jnp.where(lane_g == g, o, acc)
        o_ref[:, kvh * gw:(kvh + 1) * gw] = acc


def _swa(sink, sq, sk, sv, n_lat, n_ctx):
    t = n_lat + n_ctx
    tb = SWA_TILE
    nt = t // tb
    nlt = n_lat // tb
    kvw = SWA_KV_WIDTH
    prev_spec = pl.BlockSpec((tb, kvw), lambda i: (jnp.maximum(i - 1, 0), 0))
    own_spec = pl.BlockSpec((tb, kvw), lambda i: (i, 0))
    next_spec = pl.BlockSpec((tb, kvw), lambda i: (jnp.minimum(i + 1, nt - 1), 0))
    ctx_spec = pl.BlockSpec((n_ctx, kvw), lambda i: (n_lat // n_ctx, 0))
    return pl.pallas_call(
        functools.partial(_swa_kernel, n_lat_tiles=nlt, n_lat=n_lat, n_ctx=n_ctx),
        grid=(nt,),
        in_specs=[pl.BlockSpec(memory_space=pltpu.SMEM),
                  pl.BlockSpec((tb, SWA_WIDTH), lambda i: (i, 0)),
                  prev_spec, own_spec, next_spec, ctx_spec,
                  prev_spec, own_spec, next_spec, ctx_spec],
        out_specs=pl.BlockSpec((tb, SWA_WIDTH), lambda i: (i, 0)),
        out_shape=jax.ShapeDtypeStruct((t, SWA_WIDTH), F32),
        compiler_params=_cparams(("arbitrary",)),
        name="swa_attention",
    )(sink, sq, sk, sk, sk, sk, sv, sv, sv, sv)


def _mix_out_kernel(x_ref, mla_ref, swa_ref, cb_ref, cv_ref, hp_ref, hn_ref, cw_ref, go_ref, wo_ref,
                    mod_ref, gf_ref, wr_ref, br_ref,
                    xo_ref, h2_ref, te_ref, gt_ref, rk_ref, cnt_ref, cnt_sc,
                    *, n_lat_tiles, n_tiles):
    i = pl.program_id(0)
    tm = x_ref.shape[0]

    @pl.when(i == 0)
    def _():
        cnt_sc[...] = jnp.zeros(cnt_sc.shape, F32)

    seg_start = (i == 0) | (i == n_lat_tiles)
    seg_end = (i == n_lat_tiles - 1) | (i == n_tiles - 1)
    v = cv_ref[...]
    rowi = lax.broadcasted_iota(jnp.int32, v.shape, 0)
    left = jnp.where(seg_start, 0.0, hp_ref[SUBLANES - 1:SUBLANES, :])
    right = jnp.where(seg_end, 0.0, hn_ref[0:1, :])
    v_dn = jnp.where(rowi == 0, left, pltpu.roll(v, 1, 0))
    v_up = jnp.where(rowi == tm - 1, right, pltpu.roll(v, tm - 1, 0))
    conv = cb_ref[...] * (v_dn * cw_ref[0:1, :] + v * cw_ref[1:2, :] + v_up * cw_ref[2:3, :])

    o1, o2 = MLA_WIDTH, MLA_WIDTH + SWA_WIDTH
    ya = (_rms(mla_ref[...]) * go_ref[:, 0:o1]).astype(BF16)
    yb = (_rms(swa_ref[...]) * go_ref[:, o1:o2]).astype(BF16)
    yc = (_rms(conv) * go_ref[:, o2:MIX_WIDTH]).astype(BF16)
    o = (jnp.dot(ya, wo_ref[0:o1, :], preferred_element_type=F32)
         + jnp.dot(yb, wo_ref[o1:o2, :], preferred_element_type=F32)
         + jnp.dot(yc, wo_ref[o2:MIX_WIDTH, :], preferred_element_type=F32))
    x = x_ref[...] + mod_ref[2:3, :] * o
    xo_ref[...] = x

    h2 = _rms(x) * gf_ref[...]
    h2 = h2 * (1.0 + mod_ref[4:5, :]) + mod_ref[3:4, :]
    _store_slabs(h2_ref, h2)
    h_hi = h2.astype(BF16)
    h_lo = (h2 - h_hi.astype(F32)).astype(BF16)
    parts = (jnp.dot(h_hi, wr_ref[...], preferred_element_type=F32)
             + jnp.dot(h_lo, wr_ref[...], preferred_element_type=F32))
    logits = parts[:, 0:N_EXPERTS] + parts[:, N_EXPERTS:2 * N_EXPERTS] + br_ref[...]
    e_iota = lax.broadcasted_iota(jnp.int32, logits.shape, 1)
    lane = lax.broadcasted_iota(jnp.int32, (tm, LANES), 1)
    work = logits
    tops, idxs = [], []
    for _ in range(TOP_K):
        m = jnp.max(work, axis=-1, keepdims=True)
        idx = jnp.min(jnp.where(work == m, e_iota, N_EXPERTS), axis=-1, keepdims=True)
        tops.append(m)
        idxs.append(idx)
        work = jnp.where(e_iota == idx, -jnp.inf, work)
    exps = [jnp.exp(tk_ - tops[0]) for tk_ in tops]
    denom = exps[0] + exps[1] + exps[2] + exps[3]

    onehots = [(e_iota == idx).astype(F32) for idx in idxs]
    sel = onehots[0] + onehots[1] + onehots[2] + onehots[3]
    rr = lax.broadcasted_iota(jnp.int32, (tm, tm), 0)
    cc = lax.broadcasted_iota(jnp.int32, (tm, tm), 1)
    tri = (rr > cc).astype(BF16)
    before = jnp.dot(tri, sel.astype(BF16), preferred_element_type=F32) + cnt_sc[...]
    cnt_new = cnt_sc[...] + jnp.sum(sel, axis=0, keepdims=True)
    cnt_sc[...] = cnt_new
    cnt_ref[...] = jnp.broadcast_to(cnt_new, cnt_ref.shape).astype(jnp.int32)

    te = jnp.zeros((tm, LANES), jnp.int32)
    rk = jnp.zeros((tm, LANES), jnp.int32)
    for k in range(TOP_K):
        rank_k = jnp.sum(before * onehots[k], axis=-1, keepdims=True).astype(jnp.int32)
        te = jnp.where(lane == k, idxs[k], te)
        rk = jnp.where(lane == k, rank_k, rk)
        gt_ref[:, k * LANES:(k + 1) * LANES] = jnp.broadcast_to(exps[k] / denom, (tm, LANES))
    te_ref[...] = te
    rk_ref[...] = rk


def _mix_out(xs, mla, swa, cb, cv, conv_w, g_out, w_out_b, mod, g_ffn, w_r, b_r, n_lat_tiles):
    t, d = xs.shape
    tm = ROW_TILE
    nt = t // tm
    hb = tm // SUBLANES
    row = lambda w: pl.BlockSpec((tm, w), lambda i: (i, 0))
    full = lambda a: pl.BlockSpec(a.shape, lambda i: (0,) * a.ndim)
    return pl.pallas_call(
        functools.partial(_mix_out_kernel, n_lat_tiles=n_lat_tiles, n_tiles=nt),
        grid=(nt,),
        in_specs=[row(d), row(MLA_WIDTH), row(SWA_WIDTH), row(CONV_WIDTH), row(CONV_WIDTH),
                  pl.BlockSpec((SUBLANES, CONV_WIDTH), lambda i: (jnp.maximum(i * hb - 1, 0), 0)),
                  pl.BlockSpec((SUBLANES, CONV_WIDTH), lambda i: (jnp.minimum((i + 1) * hb, nt * hb - 1), 0)),
                  full(conv_w), full(g_out), full(w_out_b),
                  pl.BlockSpec((None, N_MOD, d), lambda i: (jnp.where(i < n_lat_tiles, 0, 1), 0, 0)),
                  full(g_ffn), full(w_r), full(b_r)],
        out_specs=[row(d), pl.BlockSpec((tm * _slab_rows(d), LANES), lambda i: (i, 0)),
                   row(LANES), row(TOP_K * LANES), row(LANES),
                   pl.BlockSpec((SUBLANES, N_EXPERTS), lambda i: (0, 0))],
        out_shape=[jax.ShapeDtypeStruct((t, d), F32),
                   jax.ShapeDtypeStruct((t * _slab_rows(d), LANES), U32),
                   jax.ShapeDtypeStruct((t, LANES), jnp.int32), jax.ShapeDtypeStruct((t, TOP_K * LANES), F32),
                   jax.ShapeDtypeStruct((t, LANES), jnp.int32),
                   jax.ShapeDtypeStruct((SUBLANES, N_EXPERTS), jnp.int32)],
        scratch_shapes=[pltpu.VMEM((1, N_EXPERTS), F32)],
        compiler_params=_cparams(("arbitrary",)),
        name="mix_out_ffn_route",
    )(xs, mla, swa, cb, cv, cv, cv, conv_w, g_out, w_out_b, mod, g_ffn, w_r, b_r)


def _dispatch_kernel(pstart_ref, pend_ref, te_ref, rk_ref, h_ref, xs_ref, zero_buf, sem, zsem, *, sr, nr):
    tm = h_ref.shape[0] // sr
    bm = zero_buf.shape[0] // sr

    @pl.when(pl.program_id(0) == 0)
    def _():
        zero_buf[...] = jnp.zeros(zero_buf.shape, U32)

        def tail_copy(e):
            last = pl.multiple_of(jnp.maximum(pend_ref[e] - bm, 0) * sr, bm * sr)
            return pltpu.make_async_copy(zero_buf, xs_ref.at[pl.ds(last, bm * sr), :], zsem)

        def zstart(e, c):
            @pl.when(pend_ref[e] > pstart_ref[e])
            def _():
                tail_copy(e).start()
            return c

        def zwait(e, c):
            @pl.when(pend_ref[e] > pstart_ref[e])
            def _():
                tail_copy(e).wait()
            return c

        lax.fori_loop(0, N_EXPERTS, zstart, 0)
        lax.fori_loop(0, N_EXPERTS, zwait, 0)

    def row_copy(r, k):
        a = r * TOP_K + k
        dest = pl.multiple_of((pstart_ref[te_ref[a]] + rk_ref[a]) * sr, sr)
        return pltpu.make_async_copy(h_ref.at[pl.ds(pl.multiple_of(r * sr, sr), nr), :],
                                     xs_ref.at[pl.ds(dest, nr), :], sem)

    def issue(r, c):
        for k in range(TOP_K):
            row_copy(r, k).start(priority=k % 2)
        return c

    def drain(r, c):
        for k in range(TOP_K):
            row_copy(r, k).wait()
        return c

    lax.fori_loop(0, tm, issue, 0)
    lax.fori_loop(0, tm, drain, 0)


def _dispatch(pstart, pend, te_flat, rk_flat, h2, n_slots, sr, nr):
    w = h2.shape[1]
    tm = ROW_TILE
    flat = pl.BlockSpec((tm * TOP_K,), lambda i, ps, pe: (i,), memory_space=pltpu.SMEM)
    grid_spec = pltpu.PrefetchScalarGridSpec(
        num_scalar_prefetch=2,
        grid=(h2.shape[0] // (tm * sr),),
        in_specs=[flat, flat, pl.BlockSpec((tm * sr, w), lambda i, ps, pe: (i, 0))],
        out_specs=pl.BlockSpec(memory_space=pl.ANY),
        scratch_shapes=[pltpu.VMEM((MOE_ROWS * sr, w), U32), pltpu.SemaphoreType.DMA(()),
                        pltpu.SemaphoreType.DMA(())],
    )
    return pl.pallas_call(
        functools.partial(_dispatch_kernel, sr=sr, nr=nr),
        grid_spec=grid_spec,
        out_shape=jax.ShapeDtypeStruct((n_slots * sr, w), U32),
        compiler_params=_cparams(("arbitrary",)),
        name="moe_dispatch",
    )(pstart, pend, te_flat, rk_flat, h2)


def _expert_kernel(be_ref, na_ref, x_ref, w1_ref, b1_ref, w2_ref, b2_ref, sel_ref, y_ref, w1b, w2b):
    b = pl.program_id(0)
    active = b < na_ref[0]
    new_expert = (b == 0) | (be_ref[b] != be_ref[jnp.maximum(b - 1, 0)])

    @pl.when(active & new_expert)
    def _():
        w1b[...] = w1_ref[...].astype(BF16)
        w2b[...] = w2_ref[...].astype(BF16)

    @pl.when(active)
    def _():
        sr = _slab_rows(w1b.shape[0])
        bm = x_ref.shape[0] // sr
        gu = b1_ref[...]
        for j in range(_slab_pieces(w1b.shape[0])):
            xj = _load_slab_piece(x_ref, j, bm, sr).astype(BF16)
            gu = gu + jnp.dot(xj, w1b[2 * j * LANES:(2 * j + 2) * LANES, :], preferred_element_type=F32)
        gate = jnp.minimum(gu, SWIGLU_LIMIT)
        a = gate * jax.nn.sigmoid(SWIGLU_ALPHA * gate)
        u = jnp.clip(gu, -SWIGLU_LIMIT, SWIGLU_LIMIT) + 1.0
        act = a * pltpu.roll(u, u.shape[1] - 1, 1)
        act = jnp.dot(act.astype(BF16), sel_ref[...], preferred_element_type=F32)
        y = jnp.dot(act.astype(BF16), w2b[...], preferred_element_type=F32) + b2_ref[...]
        _store_slabs(y_ref, y)


def _experts(layer, block_expert, n_active, xs, w1, b1, w2, b2, sel):
    d = w1.shape[2]
    sr = _slab_rows(d)
    w = xs.shape[1]
    bm = MOE_ROWS
    nb = xs.shape[0] // (bm * sr)
    blk = lambda b, be, na: (jnp.minimum(b, na[0] - 1), 0)
    wsel = lambda b, be, na: (layer, be[jnp.minimum(b, na[0] - 1)], 0, 0)
    grid_spec = pltpu.PrefetchScalarGridSpec(
        num_scalar_prefetch=2,
        grid=(nb,),
        in_specs=[pl.BlockSpec((bm * sr, w), blk),
                  pl.BlockSpec((None, None, d, 2 * D_EXPERT), wsel),
                  pl.BlockSpec((None, None, 1, 2 * D_EXPERT), wsel),
                  pl.BlockSpec((None, None, D_EXPERT, d), wsel),
                  pl.BlockSpec((None, None, 1, d), wsel),
                  pl.BlockSpec((2 * D_EXPERT, D_EXPERT), lambda b, be, na: (0, 0))],
        out_specs=pl.BlockSpec((bm * sr, w), blk),
        scratch_shapes=[pltpu.VMEM((d, 2 * D_EXPERT), BF16), pltpu.VMEM((D_EXPERT, d), BF16)],
    )
    return pl.pallas_call(
        _expert_kernel,
        grid_spec=grid_spec,
        out_shape=jax.ShapeDtypeStruct(xs.shape, U32),
        compiler_params=_cparams(("arbitrary",)),
        name="moe_experts",
    )(block_expert, n_active, xs, w1, b1, w2, b2, sel)


def _combine_kernel(pstart_ref, te_ref, rk_ref, x_ref, gt_ref, mod_ref, y_ref, *rest, final):
    if final:
        gfin_ref, o_ref, buf, sem = rest
    else:
        o_ref, buf, sem = rest
    tm, d = x_ref.shape
    sr = _slab_rows(d)
    nr = _slab_pieces(d)

    def row_copy(r, k):
        a = r * TOP_K + k
        src = pl.multiple_of((pstart_ref[te_ref[a]] + rk_ref[a]) * sr, sr)
        return pltpu.make_async_copy(y_ref.at[pl.ds(src, nr), :],
                                     buf.at[k, pl.ds(pl.multiple_of(r * sr, sr), nr), :], sem)

    def issue(r, c):
        for k in range(TOP_K):
            row_copy(r, k).start(priority=k % 2)
        return c

    def drain(r, c):
        for k in range(TOP_K):
            row_copy(r, k).wait()
        return c

    lax.fori_loop(0, tm, issue, 0)
    lax.fori_loop(0, tm, drain, 0)
    for j in range(nr):
        f = None
        for k in range(TOP_K):
            gk = gt_ref[:, k * LANES:(k + 1) * LANES]
            term = jnp.concatenate([gk, gk], axis=1) * _load_slab_piece(buf.at[k], j, tm, sr)
            f = term if f is None else f + term
        cols = slice(2 * j * LANES, (2 * j + 2) * LANES)
        o_ref[:, cols] = x_ref[:, cols] + mod_ref[5:6, cols] * f
    if final:
        o_ref[...] = _rms(o_ref[...]) * gfin_ref[...]


def _combine(pstart, te_flat, rk_flat, xs, gates, mod, y, n_lat_tiles, g_final=None):
    t, d = xs.shape
    tm = ROW_TILE
    n_tiles = t // tm if g_final is None else n_lat_tiles
    flat = pl.BlockSpec((tm * TOP_K,), lambda i, ps: (i,), memory_space=pltpu.SMEM)
    in_specs = [flat, flat,
                pl.BlockSpec((tm, d), lambda i, ps: (i, 0)),
                pl.BlockSpec((tm, TOP_K * LANES), lambda i, ps: (i, 0)),
                pl.BlockSpec((None, N_MOD, d), lambda i, ps: (jnp.where(i < n_lat_tiles, 0, 1), 0, 0)),
                pl.BlockSpec(memory_space=pl.ANY)]
    args = [pstart, te_flat, rk_flat, xs, gates, mod, y]
    if g_final is not None:
        in_specs.append(pl.BlockSpec((1, d), lambda i, ps: (0, 0)))
        args.append(g_final)
    grid_spec = pltpu.PrefetchScalarGridSpec(
        num_scalar_prefetch=1,
        grid=(n_tiles,),
        in_specs=in_specs,
        out_specs=pl.BlockSpec((tm, d), lambda i, ps: (i, 0)),
        scratch_shapes=[pltpu.VMEM((TOP_K, tm * _slab_rows(d), LANES), U32), pltpu.SemaphoreType.DMA(())],
    )
    return pl.pallas_call(
        functools.partial(_combine_kernel, final=g_final is not None),
        grid_spec=grid_spec,
        out_shape=jax.ShapeDtypeStruct((n_tiles * tm, d), F32),
        compiler_params=_cparams(("arbitrary",)),
        name="moe_combine" if g_final is None else "moe_combine_final_norm",
    )(*args)


def _rope_tables(n_lat, n_ctx):
    rows = n_lat // GRID_W
    row_id, col_id = jnp.meshgrid(jnp.arange(rows), jnp.arange(GRID_W), indexing="ij")
    half = MLA_ROPE // 2
    inv_freq = ROPE_THETA ** (-jnp.arange(0, half, 2, dtype=F32) / half)

    def axis_angles(pos):
        a = pos.reshape(-1).astype(F32)[:, None] * inv_freq[None, :]
        return jnp.concatenate([a, a], axis=-1)

    ang = jnp.concatenate([axis_angles(row_id), axis_angles(col_id)], axis=-1)
    cos, sin = jnp.cos(ang), jnp.sin(ang)
    sign = jnp.where((jnp.arange(MLA_ROPE) % 32) < 16, -1.0, 1.0).astype(F32)
    sin = sin * sign[None, :]
    cos = jnp.concatenate([cos, jnp.ones((n_ctx, MLA_ROPE), F32)], axis=0)
    sin = jnp.concatenate([sin, jnp.zeros((n_ctx, MLA_ROPE), F32)], axis=0)
    return jnp.concatenate([cos, cos], axis=1), jnp.concatenate([sin, sin], axis=1)


def _split_in_proj(w):
    d = w.shape[0]
    sizes = (MLA_Q_RANK, MLA_KV_RANK, MLA_ROPE, SWA_WIDTH, SWA_KV_WIDTH, SWA_KV_WIDTH,
             CONV_WIDTH, CONV_WIDTH, CONV_WIDTH)
    offs = np.concatenate([[0], np.cumsum(sizes)])
    part = [w[:, offs[j]:offs[j + 1]] for j in range(len(sizes))]
    cq, ckv, kr, sq, sk, sv, cb, cc, cx = part
    return jnp.concatenate([cq, ckv, sq, sk, sv, cb, cc, cx, kr, jnp.zeros((d, LANES - MLA_ROPE), w.dtype)],
                           axis=1).astype(BF16)


def _split_uq(w):
    r = w.shape[0]
    w = w.reshape(r, MLA_HEADS, MLA_NOPE + MLA_ROPE)
    wn = w[:, :, :MLA_NOPE].reshape(r, MLA_HEADS * MLA_NOPE)
    wr = jnp.concatenate([w[:, :, MLA_NOPE:], jnp.zeros((r, MLA_HEADS, LANES - MLA_ROPE), w.dtype)], axis=2)
    return wn.astype(BF16), wr.reshape(r, MLA_HEADS * LANES).astype(BF16)


def _split_hi_lo(w):
    hi = w.astype(BF16)
    lo = (w - hi.astype(F32)).astype(BF16)
    return jnp.concatenate([hi, lo], axis=1)


def kernel(x, c, ctx, c_ctx, w_ada, b_ada, g_mix, w_in, g_mla_q, g_mla_kv, w_mla_uq, w_mla_ukv,
           swa_sink, conv_w, g_out, w_out, g_ffn, w_router, b_router, w_exp1, b_exp1, w_exp2,
           b_exp2, g_final):
    bsz, n_lat, d = x.shape
    n_ctx = ctx.shape[1]
    depth = w_ada.shape[0]
    assert bsz == 1 and n_lat % ATTN_K_CHUNK == 0 and n_ctx == ROW_TILE and d % LANES == 0
    t = n_lat + n_ctx
    t_pad = -(-t // ATTN_K_CHUNK) * ATTN_K_CHUNK
    n_lat_tiles = n_lat // ROW_TILE

    xs = jnp.concatenate([x[0], ctx[0]], axis=0)
    cos, sin = _rope_tables(n_lat, n_ctx)
    mods = _ada(c, c_ctx, w_ada, b_ada).reshape(depth, 2, N_MOD, d)

    n_blocks = -(-t * TOP_K // MOE_ROWS) + N_EXPERTS
    n_slots = n_blocks * MOE_ROWS
    even_sel = (jnp.arange(2 * D_EXPERT)[:, None] == 2 * jnp.arange(D_EXPERT)[None, :]).astype(BF16)

    for l in range(depth):
        mod = mods[l]
        cq, ckv, kr, sq, sk, sv, cb, cv = _inproj(xs, g_mix[l][None], mod, cos, sin,
                                                  _split_in_proj(w_in[l]), n_lat_tiles)
        wqn, wqr = _split_uq(w_mla_uq[l])
        q, k, v = _mla_up(cq, ckv, kr, g_mla_q[l][None], g_mla_kv[l][None], cos, sin,
                          wqn, wqr, w_mla_ukv[l].astype(BF16), t_pad)
        mla = _mla_attention(q, k, v, n_lat, n_ctx)
        swa = _swa(swa_sink[l], sq, sk, sv, n_lat, n_ctx)
        xs, h2, top_e, gates, rank, counts = _mix_out(
            xs, mla, swa, cb, cv, conv_w[l], g_out[l][None], w_out[l].astype(BF16), mod,
            g_ffn[l][None], _split_hi_lo(w_router[l]), b_router[l][None], n_lat_tiles)

        cnt = counts[0]
        padded = (cnt + MOE_ROWS - 1) // MOE_ROWS * MOE_ROWS
        pend = jnp.cumsum(padded).astype(jnp.int32)
        pstart = (pend - padded).astype(jnp.int32)
        bstart = jnp.arange(n_blocks, dtype=jnp.int32) * MOE_ROWS
        block_expert = jnp.minimum(jnp.sum((pend[None, :] <= bstart[:, None]).astype(jnp.int32), axis=1),
                                   N_EXPERTS - 1).astype(jnp.int32)
        n_active = (pend[-1:] // MOE_ROWS).astype(jnp.int32)
        te_flat = top_e[:, :TOP_K].reshape(-1)
        rk_flat = rank[:, :TOP_K].reshape(-1)

        xg = _dispatch(pstart, pend, te_flat, rk_flat, h2, n_slots, _slab_rows(d), _slab_pieces(d))
        y = _experts(l, block_expert, n_active, xg, w_exp1, b_exp1[:, :, None, :],
                     w_exp2, b_exp2[:, :, None, :], even_sel)
        xs = _combine(pstart, te_flat, rk_flat, xs, gates, mod, y, n_lat_tiles,
                      g_final=g_final[None] if l == depth - 1 else None)

    return xs.reshape(1, n_lat, d)
```
